```python
import math
import jax, jax.numpy as jnp
from jax import lax
import numpy as np

D_MODEL = 1024
BATCH = 1
SEQ = 16384
DEPTH = 2
DEC_BATCH = 32
DEC_SEQ = 64
PAST_LEN = 1024

CHUNK = 64
Q_BLOCK = 128
N_MEM = 256
EPS = 1e-6
ROPE_THETA = 10000.0
NEG = -1e30
F32 = jnp.float32

MLA_HEADS = 8
MLA_NOPE = 128
MLA_ROPE = 64
MLA_V = 128
MLA_Q_RANK = 384
MLA_KV_RANK = 256
MLA_IN = MLA_Q_RANK + MLA_KV_RANK + MLA_ROPE
MLA_OUT = MLA_HEADS * MLA_V

HG_HEADS = 8
HG_DIM = 128
HG_WIDTH = HG_HEADS * HG_DIM

X_HEADS = 4
X_DIM = 128
X_WIDTH = X_HEADS * X_DIM

D_FF = 4 * D_MODEL

N_MLA_LAYERS = (DEPTH + 1) // 2
N_HGRN_LAYERS = DEPTH // 2
MIX_OUT = MLA_OUT + X_WIDTH

kernel_name = 'hybrid_mla_hgrn2_stream_step'


def rmsnorm(x, g):
    xf = x.astype(F32)
    y = xf * lax.rsqrt(jnp.mean(xf * xf, axis=-1, keepdims=True) + EPS)
    return (y * g.astype(F32)).astype(x.dtype)


def rope(x, pos):
    half = x.shape[-1] // 2
    inv = jnp.power(ROPE_THETA, -jnp.arange(half, dtype=F32) / half)
    ang = pos.astype(F32)[:, None] * inv[None, :]
    ang = ang.reshape((1, ang.shape[0]) + (1,) * (x.ndim - 3) + (half,))
    cos, sin = jnp.cos(ang), jnp.sin(ang)
    xf = x.astype(F32)
    x1, x2 = xf[..., :half], xf[..., half:]
    return jnp.concatenate([x1 * cos - x2 * sin, x2 * cos + x1 * sin], axis=-1).astype(x.dtype)


def mla_attention(q_lat, q_rope, lat_all, krope_all, q_pos):
    B, T, H, C = q_lat.shape
    L = lat_all.shape[1]
    k_chunk = jnp.arange(L) // CHUNK
    scale = (MLA_NOPE + MLA_ROPE) ** -0.5

    def block(args):
        ql, qr, qp = args
        s = (jnp.einsum('bqhc,bkc->bhqk', ql, lat_all)
             + jnp.einsum('bqhr,bkr->bhqk', qr, krope_all)).astype(F32) * scale
        mask = k_chunk[None, :] <= (qp // CHUNK)[:, None]
        p = jax.nn.softmax(jnp.where(mask, s, NEG), axis=-1).astype(lat_all.dtype)
        return jnp.einsum('bhqk,bkc->bqhc', p, lat_all)

    qb = Q_BLOCK if T % Q_BLOCK == 0 else T
    nb = T // qb
    if nb == 1:
        return block((q_lat, q_rope, q_pos))
    split = lambda a: jnp.moveaxis(a.reshape((B, nb, qb) + a.shape[2:]), 1, 0)
    out = lax.map(block, (split(q_lat), split(q_rope), q_pos.reshape(nb, qb)))
    return jnp.moveaxis(out, 0, 1).reshape(B, T, H, C)


def memory_attention(xq, mem_k, mem_v):
    s = jnp.einsum('bthd,bmhd->bhtm', xq, mem_k).astype(F32) * (X_DIM ** -0.5)
    p = jax.nn.softmax(s, axis=-1).astype(mem_v.dtype)
    return jnp.einsum('bhtm,bmhd->bthd', p, mem_v)


def gated_linear_recurrence(q, k, v, logf, s0, chunk):
    B, T, H, K = q.shape
    V = v.shape[-1]
    n = T // chunk
    blk = lambda a: a.astype(F32).reshape((B, n, chunk) + a.shape[2:])
    q, k, v, logf = blk(q), blk(k), blk(v), blk(logf)
    b = jnp.cumsum(logf, axis=2)
    ref = b[:, :, chunk // 2][:, :, None]
    b_last = b[:, :, -1]
    causal = jnp.tril(jnp.ones((chunk, chunk), dtype=bool))
    a = jnp.einsum('bnthk,bnshk->bnhts', q * jnp.exp(b - ref), k * jnp.exp(ref - b))
    a = jnp.where(causal, a, 0.0)
    o_intra = jnp.einsum('bnhts,bnshv->bnthv', a, v)
    u = jnp.einsum('bnshk,bnshv->bnhkv', k * jnp.exp(b_last[:, :, None] - b), v)
    decay = jnp.exp(b_last)

    def step(s, xs):
        d, uc = xs
        return d[..., None] * s + uc, s

    s_fin, s_in = lax.scan(step, s0.astype(F32),
                           (jnp.moveaxis(decay, 1, 0), jnp.moveaxis(u, 1, 0)))
    o_inter = jnp.einsum('bnthk,nbhkv->bnthv', q * jnp.exp(b), s_in)
    o = (o_intra + o_inter).reshape(B, T, H, V)
    return o, s_fin


def hgrn2_mixer(proj, lb, s0, o_norm_g):
    B, T, _ = proj.shape
    q, f, i, g = jnp.split(proj, 4, axis=-1)
    heads = lambda a: a.reshape(B, T, HG_HEADS, HG_DIM)
    ff = f.astype(F32)
    lbf = lb.astype(F32)
    forget = lbf + (1.0 - lbf) * jax.nn.sigmoid(ff)
    logf = jnp.log(forget)
    k = (1.0 - lbf) * jax.nn.sigmoid(-ff)
    chunk = CHUNK if T % CHUNK == 0 else T
    o, s_fin = gated_linear_recurrence(heads(jax.nn.silu(q)), heads(k), heads(i), heads(logf), s0, chunk)
    o = rmsnorm(o.astype(proj.dtype), o_norm_g) * heads(jax.nn.silu(g))
    return o.reshape(B, T, HG_WIDTH), s_fin.astype(s0.dtype)


def trunk(x, pos, past_lat, past_krope, hgrn_s0, mem_k, mem_v,
          ln_mix_pre, ln_mix_post, ln_ffn_pre, ln_ffn_post,
          mla_w_in, mla_q_norm, mla_kv_norm, mla_w_uq, mla_w_uk, mla_w_uv, mla_w_out,
          hgrn_w_in, hgrn_lb, hgrn_o_norm, hgrn_w_out, w_ffn_up, w_ffn_down):
    B, T, _ = x.shape
    lb_soft = jax.nn.softmax(hgrn_lb.astype(F32), axis=0)
    lb_all = jnp.cumsum(lb_soft, axis=0) - lb_soft[0]
    new_lat, new_krope, new_state = [], [], []
    for l in range(DEPTH):
        j = l // 2
        h = rmsnorm(x, ln_mix_pre[l])
        if l % 2 == 0:
            proj = h @ mla_w_in[j]
            c_q = proj[..., :MLA_Q_RANK]
            c_kv = proj[..., MLA_Q_RANK:MLA_Q_RANK + MLA_KV_RANK]
            k_r = proj[..., MLA_Q_RANK + MLA_KV_RANK:MLA_IN]
            xq = proj[..., MLA_IN:]
            q = (rmsnorm(c_q, mla_q_norm[j]) @ mla_w_uq[j]).reshape(B, T, MLA_HEADS, MLA_NOPE + MLA_ROPE)
            q_nope = q[..., :MLA_NOPE]
            q_rope = rope(q[..., MLA_NOPE:], pos)
            lat = rmsnorm(c_kv, mla_kv_norm[j])
            krope = rope(k_r, pos)
            q_lat = jnp.einsum('bthd,chd->bthc', q_nope, mla_w_uk[j])
            if past_lat is None:
                lat_all, krope_all = lat, krope
            else:
                lat_all = jnp.concatenate([past_lat[j], lat], axis=1)
                krope_all = jnp.concatenate([past_krope[j], krope], axis=1)
            o_lat = mla_attention(q_lat, q_rope, lat_all, krope_all, pos)
            mix = jnp.einsum('bthc,chv->bthv', o_lat, mla_w_uv[j]).reshape(B, T, MLA_OUT)
            w_out = mla_w_out[j]
            new_lat.append(lat)
            new_krope.append(krope)
        else:
            proj = h @ hgrn_w_in[j]
            mix, s_fin = hgrn2_mixer(proj[..., :4 * HG_WIDTH], lb_all[l], hgrn_s0[j], hgrn_o_norm[j])
            xq = proj[..., 4 * HG_WIDTH:]
            w_out = hgrn_w_out[j]
            new_state.append(s_fin)
        cross = memory_attention(xq.reshape(B, T, X_HEADS, X_DIM), mem_k[l], mem_v[l]).reshape(B, T, X_WIDTH)
        x = x + rmsnorm(jnp.concatenate([mix, cross], axis=-1) @ w_out, ln_mix_post[l])
        h = rmsnorm(x, ln_ffn_pre[l])
        x = x + rmsnorm(jnp.square(jax.nn.relu(h @ w_ffn_up[l])) @ w_ffn_down[l], ln_ffn_post[l])
    return x, jnp.stack(new_lat), jnp.stack(new_krope), jnp.stack(new_state)


def setup_inputs(seed: int = 0) -> dict:
    key = jax.random.key(seed)
    ks = jax.random.split(key, 32)
    nrm = lambda k, shape, scale: jax.random.normal(k, shape, F32) * scale
    gain = lambda k, shape: 1.0 + 0.05 * jax.random.normal(k, shape, F32)
    return {
        'x_prompt': nrm(ks[0], (BATCH, SEQ, D_MODEL), 1.0),
        'x_sample': nrm(ks[1], (DEC_BATCH, DEC_SEQ, D_MODEL), 1.0),
        'cache_mla_latent': nrm(ks[2], (N_MLA_LAYERS, DEC_BATCH, PAST_LEN, MLA_KV_RANK), 1.0),
        'cache_mla_krope': nrm(ks[3], (N_MLA_LAYERS, DEC_BATCH, PAST_LEN, MLA_ROPE), 1.0),
        'cache_hgrn_state': nrm(ks[4], (N_HGRN_LAYERS, DEC_BATCH, HG_HEADS, HG_DIM, HG_DIM), 0.3),
        'cache_mem_k': nrm(ks[5], (DEPTH, DEC_BATCH, N_MEM, X_HEADS, X_DIM), 1.0),
        'cache_mem_v': nrm(ks[6], (DEPTH, DEC_BATCH, N_MEM, X_HEADS, X_DIM), 1.0),
        'mem_prompt': nrm(ks[7], (BATCH, N_MEM, D_MODEL), 1.0),
        'ln_mix_pre': gain(ks[8], (DEPTH, D_MODEL)),
        'ln_mix_post': gain(ks[9], (DEPTH, D_MODEL)),
        'ln_ffn_pre': gain(ks[10], (DEPTH, D_MODEL)),
        'ln_ffn_post': gain(ks[11], (DEPTH, D_MODEL)),
        'mem_norm': gain(ks[12], (DEPTH, D_MODEL)),
        'w_mem_kv': nrm(ks[13], (DEPTH, D_MODEL, 2 * X_WIDTH), D_MODEL ** -0.5),
        'mla_w_in': nrm(ks[14], (N_MLA_LAYERS, D_MODEL, MLA_IN + X_WIDTH), D_MODEL ** -0.5),
        'mla_q_norm': gain(ks[15], (N_MLA_LAYERS, MLA_Q_RANK)),
        'mla_kv_norm': gain(ks[16], (N_MLA_LAYERS, MLA_KV_RANK)),
        'mla_w_uq': nrm(ks[17], (N_MLA_LAYERS, MLA_Q_RANK, MLA_HEADS * (MLA_NOPE + MLA_ROPE)), MLA_Q_RANK ** -0.5),
        'mla_w_uk': nrm(ks[18], (N_MLA_LAYERS, MLA_KV_RANK, MLA_HEADS, MLA_NOPE), MLA_KV_RANK ** -0.5),
        'mla_w_uv': nrm(ks[19], (N_MLA_LAYERS, MLA_KV_RANK, MLA_HEADS, MLA_V), MLA_KV_RANK ** -0.5),
        'mla_w_out': nrm(ks[20], (N_MLA_LAYERS, MIX_OUT, D_MODEL), MIX_OUT ** -0.5),
        'hgrn_w_in': nrm(ks[21], (N_HGRN_LAYERS, D_MODEL, 4 * HG_WIDTH + X_WIDTH), D_MODEL ** -0.5),
        'hgrn_lb': nrm(ks[22], (DEPTH, HG_WIDTH), 0.1),
        'hgrn_o_norm': gain(ks[23], (N_HGRN_LAYERS, HG_DIM)),
        'hgrn_w_out': nrm(ks[24], (N_HGRN_LAYERS, MIX_OUT, D_MODEL), MIX_OUT ** -0.5),
        'w_ffn_up': nrm(ks[25], (DEPTH, D_MODEL, D_FF), D_MODEL ** -0.5),
        'w_ffn_down': nrm(ks[26], (DEPTH, D_FF, D_MODEL), D_FF ** -0.5),
    }


def reference(x_prompt, x_sample, cache_mla_latent, cache_mla_krope, cache_hgrn_state,
              cache_mem_k, cache_mem_v, mem_prompt,
              ln_mix_pre, ln_mix_post, ln_ffn_pre, ln_ffn_post, mem_norm, w_mem_kv,
              mla_w_in, mla_q_norm, mla_kv_norm, mla_w_uq, mla_w_uk, mla_w_uv, mla_w_out,
              hgrn_w_in, hgrn_lb, hgrn_o_norm, hgrn_w_out, w_ffn_up, w_ffn_down):
    Bp, Tp, _ = x_prompt.shape
    Bs, Ts, _ = x_sample.shape
    past = cache_mla_latent.shape[2]
    mem_n = rmsnorm(mem_prompt[None], mem_norm[:, None, None, :])
    kv = jnp.einsum('lbmd,lde->lbme', mem_n, w_mem_kv).reshape(DEPTH, Bp, N_MEM, 2, X_HEADS, X_DIM)
    mem_k_p, mem_v_p = kv[:, :, :, 0], kv[:, :, :, 1]
    weights = (ln_mix_pre, ln_mix_post, ln_ffn_pre, ln_ffn_post,
               mla_w_in, mla_q_norm, mla_kv_norm, mla_w_uq, mla_w_uk, mla_w_uv, mla_w_out,
               hgrn_w_in, hgrn_lb, hgrn_o_norm, hgrn_w_out, w_ffn_up, w_ffn_down)
    s0_p = jnp.zeros((N_HGRN_LAYERS, Bp, HG_HEADS, HG_DIM, HG_DIM), x_prompt.dtype)
    y_p, lat_p, kr_p, st_p = trunk(x_prompt, jnp.arange(Tp), None, None, s0_p,
                                   mem_k_p, mem_v_p, *weights)
    y_s, lat_s, kr_s, st_s = trunk(x_sample, past + jnp.arange(Ts), cache_mla_latent, cache_mla_krope,
                                   cache_hgrn_state, cache_mem_k, cache_mem_v, *weights)
    return (y_p, y_s, lat_p, kr_p, st_p, mem_k_p, mem_v_p, lat_s, kr_s, st_s)
```

```python
import functools

import jax
import jax.numpy as jnp
from jax import lax
from jax.experimental import pallas as pl
from jax.experimental.pallas import tpu as pltpu

F32 = jnp.float32
BF16 = jnp.bfloat16

CHUNK = 64
EPS = 1e-6
ROPE_THETA = 10000.0
NEG = -1e30
MLA_HEADS = 8
MLA_NOPE = 128
MLA_ROPE = 64
MLA_V = 128
MLA_Q_RANK = 384
MLA_KV_RANK = 256
HG_HEADS = 8
HG_DIM = 128
X_HEADS = 4
X_DIM = 128
LANES = 128

VMEM_LIMIT_BYTES = 52 * 1024 * 1024


def _dot(a, b):
    return jnp.dot(a, b, preferred_element_type=F32)


def _dot_nt(a, b):
    return lax.dot_general(a, b, (((1,), (1,)), ((), ())), preferred_element_type=F32)


def _dot_tn(a, b):
    return lax.dot_general(a, b, (((0,), (0,)), ((), ())), preferred_element_type=F32)


def _rms(x, g):
    ms = jnp.mean(x * x, axis=-1, keepdims=True)
    return x * lax.rsqrt(ms + EPS) * g


def _sigmoid(x):
    return 1.0 / (1.0 + jnp.exp(-x))


def _params(*sem):
    return pltpu.CompilerParams(dimension_semantics=sem, vmem_limit_bytes=VMEM_LIMIT_BYTES)


def _const_spec(shape):
    nd = len(shape)
    return pl.BlockSpec(shape, lambda *_: (0,) * nd, pipeline_mode=pl.Buffered(1))


def _memkv_kernel(mem_ref, g_ref, w_ref, k_ref, v_ref):
    h = _rms(mem_ref[0], g_ref[0]).astype(BF16)
    kv = _dot(h, w_ref[0])
    half = kv.shape[1] // 2
    k_ref[0, 0] = kv[:, :half]
    v_ref[0, 0] = kv[:, half:]


def _memkv(mem, g, w):
    bp, n_mem, d = mem.shape
    depth = g.shape[0]
    width = w.shape[2] // 2
    out = jax.ShapeDtypeStruct((depth, bp, n_mem, width), F32)
    return pl.pallas_call(
        _memkv_kernel,
        grid=(depth, bp),
        in_specs=[
            pl.BlockSpec((1, n_mem, d), lambda l, b: (b, 0, 0)),
            pl.BlockSpec((1, 1, d), lambda l, b: (l, 0, 0)),
            pl.BlockSpec((1, d, 2 * width), lambda l, b: (l, 0, 0)),
        ],
        out_specs=[
            pl.BlockSpec((1, 1, n_mem, width), lambda l, b: (l, b, 0, 0)),
            pl.BlockSpec((1, 1, n_mem, width), lambda l, b: (l, b, 0, 0)),
        ],
        out_shape=[out, out],
        compiler_params=_params("arbitrary", "arbitrary"),
        name="memkv",
    )(mem, g.reshape(depth, 1, d), w)


def _rope_rot(x, c, s1, s2):
    w = x.shape[1]
    half = MLA_ROPE // 2
    return x * c + pltpu.roll(x, half, 1) * s1 + pltpu.roll(x, w - half, 1) * s2


def _pre0_kernel(x_ref, g_ref, win_ref, qg_ref, kvg_ref, wuq_ref, wuk_ref, wuv_ref,
                 c_ref, s1_ref, s2_ref,
                 qn_ref, qr_ref, kn_ref, kr_ref, v_ref, lat_ref, krope_ref, xq_ref):
    h = _rms(x_ref[...], g_ref[...]).astype(BF16)
    proj = _dot(h, win_ref[...])
    o1 = MLA_Q_RANK
    o2 = o1 + MLA_KV_RANK
    o3 = o2 + X_HEADS * X_DIM
    c_q, c_kv, xq, k_r = proj[:, :o1], proj[:, o1:o2], proj[:, o2:o3], proj[:, o3:]

    scale = (MLA_NOPE + MLA_ROPE) ** -0.5
    q = _dot(_rms(c_q, qg_ref[...]).astype(BF16), wuq_ref[...]) * scale
    nope_w = MLA_HEADS * MLA_NOPE
    qn_ref[...] = q[:, :nope_w].astype(BF16)
    c, s1, s2 = c_ref[...], s1_ref[...], s2_ref[...]
    tile = lambda t: jnp.concatenate([t] * MLA_HEADS, axis=1)
    qr_ref[...] = _rope_rot(q[:, nope_w:], tile(c), tile(s1), tile(s2)).astype(BF16)

    lat = _rms(c_kv, kvg_ref[...])
    lat_ref[...] = lat
    latb = lat.astype(BF16)
    kn_ref[...] = _dot(latb, wuk_ref[...]).astype(BF16)
    v_ref[...] = _dot(latb, wuv_ref[...]).astype(BF16)

    kr = _rope_rot(k_r, c, s1, s2)
    krope_ref[...] = kr[:, :MLA_ROPE]
    kr_ref[...] = kr.astype(BF16)
    xq_ref[...] = (xq * X_DIM ** -0.5).astype(BF16)


def _pre0(x, g, win, qg, kvg, wuq, wuk, wuv, c, s1, s2, tm):
    n, d = x.shape
    hw = MLA_HEADS * LANES
    xw = X_HEADS * X_DIM
    row = lambda w: pl.BlockSpec((tm, w), lambda i: (i, 0))
    return pl.pallas_call(
        _pre0_kernel,
        grid=(n // tm,),
        in_specs=[row(d), _const_spec(g.shape), _const_spec(win.shape), _const_spec(qg.shape),
                  _const_spec(kvg.shape), _const_spec(wuq.shape), _const_spec(wuk.shape),
                  _const_spec(wuv.shape), row(LANES), row(LANES), row(LANES)],
        out_specs=[row(hw), row(hw), row(hw), row(LANES), row(hw),
                   row(MLA_KV_RANK), row(MLA_ROPE), row(xw)],
        out_shape=[
            jax.ShapeDtypeStruct((n, hw), BF16),
            jax.ShapeDtypeStruct((n, hw), BF16),
            jax.ShapeDtypeStruct((n, hw), BF16),
            jax.ShapeDtypeStruct((n, LANES), BF16),
            jax.ShapeDtypeStruct((n, hw), BF16),
            jax.ShapeDtypeStruct((n, MLA_KV_RANK), F32),
            jax.ShapeDtypeStruct((n, MLA_ROPE), F32),
            jax.ShapeDtypeStruct((n, xw), BF16),
        ],
        compiler_params=_params("arbitrary"),
        name="pre0",
    )(x, g, win, qg, kvg, wuq, wuk, wuv, c, s1, s2)


def _expand_kernel(lat_ref, kr_ref, wuk_ref, wuv_ref, kn_ref, krp_ref, v_ref):
    latb = lat_ref[...].astype(BF16)
    kn_ref[...] = _dot(latb, wuk_ref[...]).astype(BF16)
    v_ref[...] = _dot(latb, wuv_ref[...]).astype(BF16)
    kr = kr_ref[...].astype(BF16)
    krp_ref[...] = jnp.concatenate([kr, jnp.zeros_like(kr)], axis=1)


def _expand(lat, kr, wuk, wuv, tm):
    n = lat.shape[0]
    hw = MLA_HEADS * LANES
    row = lambda w: pl.BlockSpec((tm, w), lambda i: (i, 0))
    return pl.pallas_call(
        _expand_kernel,
        grid=(n // tm,),
        in_specs=[row(MLA_KV_RANK), row(MLA_ROPE), _const_spec(wuk.shape), _const_spec(wuv.shape)],
        out_specs=[row(hw), row(LANES), row(hw)],
        out_shape=[jax.ShapeDtypeStruct((n, hw), BF16),
                   jax.ShapeDtypeStruct((n, LANES), BF16),
                   jax.ShapeDtypeStruct((n, hw), BF16)],
        compiler_params=_params("arbitrary"),
        name="expand_cache",
    )(lat, kr, wuk, wuv)


def _attn_p_kernel(qi_ref, ki_ref, qn_ref, qr_ref, kn_ref, kr_ref, v_ref, o_ref,
                   m_ref, l_ref, acc_ref, *, tq, tk):
    s_id = pl.program_id(0)
    qi = qi_ref[s_id]
    ki = ki_ref[s_id]
    rep = tk // LANES

    @pl.when(ki == 0)
    def _():
        m_ref[...] = jnp.full(m_ref.shape, NEG, F32)
        l_ref[...] = jnp.zeros(l_ref.shape, F32)
        acc_ref[...] = jnp.zeros(acc_ref.shape, F32)

    def step(masked):
        if masked:
            row = lax.broadcasted_iota(jnp.int32, (tq, tk), 0) // CHUNK
            col = lax.broadcasted_iota(jnp.int32, (tq, tk), 1) // CHUNK
            keep = col <= row
        kr = kr_ref[...]
        for h in range(MLA_HEADS):
            sl = slice(h * LANES, (h + 1) * LANES)
            q_h = jnp.concatenate([qn_ref[:, sl], qr_ref[:, sl]], axis=1)
            k_h = jnp.concatenate([kn_ref[:, sl], kr], axis=1)
            s = _dot_nt(q_h, k_h)
            if masked:
                s = jnp.where(keep, s, NEG)
            m_prev = m_ref[h]
            m_new = jnp.maximum(m_prev, jnp.max(s, axis=1, keepdims=True))
            alpha = jnp.exp(m_prev - m_new)
            p = jnp.exp(s - jnp.concatenate([m_new] * rep, axis=1))
            l_ref[h] = alpha * l_ref[h] + jnp.sum(p, axis=1, keepdims=True)
            acc_ref[h] = alpha * acc_ref[h] + _dot(p.astype(BF16), v_ref[:, sl])
            m_ref[h] = m_new

    @pl.when(ki < qi)
    def _():
        step(False)

    @pl.when(ki == qi)
    def _():
        step(True)
        for h in range(MLA_HEADS):
            sl = slice(h * LANES, (h + 1) * LANES)
            o_ref[:, sl] = (acc_ref[h] / l_ref[h]).astype(BF16)


def _attn_prompt(qn, qr, kn, kr, v, t):
    n, hw = qn.shape
    nb = n // t
    qi_l, ki_l = [], []
    for qi in range(nb):
        for ki in range(qi + 1):
            qi_l.append(qi)
            ki_l.append(ki)
    qi_tbl = jnp.asarray(qi_l, jnp.int32)
    ki_tbl = jnp.asarray(ki_l, jnp.int32)
    qspec = lambda w: pl.BlockSpec((t, w), lambda s, qi, ki: (qi[s], 0))
    kspec = lambda w: pl.BlockSpec((t, w), lambda s, qi, ki: (ki[s], 0))
    stat = pltpu.VMEM((MLA_HEADS, t, LANES), F32)
    return pl.pallas_call(
        functools.partial(_attn_p_kernel, tq=t, tk=t),
        grid_spec=pltpu.PrefetchScalarGridSpec(
            num_scalar_prefetch=2,
            grid=(len(qi_l),),
            in_specs=[qspec(hw), qspec(hw), kspec(hw), kspec(LANES), kspec(hw)],
            out_specs=qspec(hw),
            scratch_shapes=[stat, stat, stat],
        ),
        out_shape=jax.ShapeDtypeStruct((n, hw), BF16),
        compiler_params=_params("arbitrary"),
        name="attn_prompt",
    )(qi_tbl, ki_tbl, qn, qr, kn, kr, v)


def _attn_s_kernel(qn_ref, qr_ref, knp_ref, krp_ref, vp_ref, kn_ref, kr_ref, v_ref, o_ref,
                   *, past, ts):
    q_chunk = (past + lax.broadcasted_iota(jnp.int32, (ts, 1), 0)) // CHUNK
    keep_p = (lax.broadcasted_iota(jnp.int32, (ts, past), 1) // CHUNK) <= q_chunk
    keep_n = ((past + lax.broadcasted_iota(jnp.int32, (ts, ts), 1)) // CHUNK) <= q_chunk
    krp = krp_ref[0]
    kr = kr_ref[...]
    for h in range(MLA_HEADS):
        sl = slice(h * LANES, (h + 1) * LANES)
        q_h = jnp.concatenate([qn_ref[:, sl], qr_ref[:, sl]], axis=1)
        s_p = _dot_nt(q_h, jnp.concatenate([knp_ref[0, :, sl], krp], axis=1))
        s_n = _dot_nt(q_h, jnp.concatenate([kn_ref[:, sl], kr], axis=1))
        s_p = jnp.where(keep_p, s_p, NEG)
        s_n = jnp.where(keep_n, s_n, NEG)
        m = jnp.maximum(jnp.max(s_p, axis=1, keepdims=True), jnp.max(s_n, axis=1, keepdims=True))
        p_p = jnp.exp(s_p - m)
        p_n = jnp.exp(s_n - m)
        l = jnp.sum(p_p, axis=1, keepdims=True) + jnp.sum(p_n, axis=1, keepdims=True)
        o = _dot(p_p.astype(BF16), vp_ref[0, :, sl]) + _dot(p_n.astype(BF16), v_ref[:, sl])
        o_ref[:, sl] = (o / l).astype(BF16)


def _attn_sample(qn, qr, knp, krp, vp, kn, kr, v, ts):
    n, hw = qn.shape
    nb, past, _ = knp.shape
    new = lambda w: pl.BlockSpec((ts, w), lambda b: (b, 0))
    old = lambda w: pl.BlockSpec((1, past, w), lambda b: (b, 0, 0))
    return pl.pallas_call(
        functools.partial(_attn_s_kernel, past=past, ts=ts),
        grid=(nb,),
        in_specs=[new(hw), new(hw), old(hw), old(LANES), old(hw), new(hw), new(LANES), new(hw)],
        out_specs=new(hw),
        out_shape=jax.ShapeDtypeStruct((n, hw), BF16),
        compiler_params=_params("arbitrary"),
        name="attn_sample",
    )(qn, qr, knp, krp, vp, kn, kr, v)


def _post_kernel(x_ref, mix_ref, xq_ref, mk_ref, mv_ref, wo_ref, g1_ref, g2_ref, wup_ref,
                 wdn_ref, g3_ref, y_ref, *, nb, tb):
    cross = []
    for b in range(nb):
        rows = slice(b * tb, (b + 1) * tb)
        heads = []
        for h in range(X_HEADS):
            sl = slice(h * X_DIM, (h + 1) * X_DIM)
            s = _dot_nt(xq_ref[rows, sl], mk_ref[b, :, sl])
            p = jnp.exp(s - jnp.max(s, axis=1, keepdims=True))
            l = jnp.sum(p, axis=1, keepdims=True)
            heads.append((_dot(p.astype(BF16), mv_ref[b, :, sl]) / l).astype(BF16))
        cross.append(jnp.concatenate(heads, axis=1))
    cross = cross[0] if nb == 1 else jnp.concatenate(cross, axis=0)
    cat = jnp.concatenate([mix_ref[...], cross], axis=1)
    x1 = x_ref[...] + _rms(_dot(cat, wo_ref[...]), g1_ref[...])

    hb = _rms(x1, g2_ref[...]).astype(BF16)
    d_ff = wup_ref.shape[1]
    step = 1024
    acc = None
    for c in range(d_ff // step):
        up = jnp.maximum(_dot(hb, wup_ref[:, c * step:(c + 1) * step]), 0.0)
        dn = _dot((up * up).astype(BF16), wdn_ref[c * step:(c + 1) * step, :])
        acc = dn if acc is None else acc + dn
    y_ref[...] = x1 + _rms(acc, g3_ref[...])


def _post(x, mix, xq, mk, mv, wo, g1, g2, wup, wdn, g3, tm, tb):
    n, d = x.shape
    nb = tm // tb
    n_mem, xw = mk.shape[1], mk.shape[2]
    row = lambda w: pl.BlockSpec((tm, w), lambda i: (i, 0))
    if mk.shape[0] * tb == n:
        mem = pl.BlockSpec((nb, n_mem, xw), lambda i: (i, 0, 0))
    else:
        assert mk.shape[0] == 1 and nb == 1
        mem = pl.BlockSpec((1, n_mem, xw), lambda i: (0, 0, 0))
    return pl.pallas_call(
        functools.partial(_post_kernel, nb=nb, tb=tb),
        grid=(n // tm,),
        in_specs=[row(d), row(mix.shape[1]), row(xw), mem, mem, _const_spec(wo.shape),
                  _const_spec(g1.shape), _const_spec(g2.shape), _const_spec(wup.shape),
                  _const_spec(wdn.shape), _const_spec(g3.shape)],
        out_specs=row(d),
        out_shape=jax.ShapeDtypeStruct((n, d), F32),
        compiler_params=_params("arbitrary"),
        name="post",
    )(x, mix, xq, mk, mv, wo, g1, g2, wup, wdn, g3)


def _pre1_kernel(x_ref, g_ref, w_ref, lb_ref,
                 qs_ref, lf_ref, kk_ref, v_ref, gs_ref, xq_ref, *, layer):
    hb = _rms(x_ref[...], g_ref[...]).astype(BF16)
    proj = _dot(hb, w_ref[...])
    w = HG_HEADS * HG_DIM
    q, f, i, g, xq = (proj[:, :w], proj[:, w:2 * w], proj[:, 2 * w:3 * w],
                      proj[:, 3 * w:4 * w], proj[:, 4 * w:])
    lb_all = lb_ref[...]
    e = jnp.exp(lb_all - jnp.max(lb_all, axis=0, keepdims=True))
    soft = e / jnp.sum(e, axis=0, keepdims=True)
    lb = jnp.sum(soft[:layer + 1], axis=0, keepdims=True) - soft[0:1]
    forget = lb + (1.0 - lb) * _sigmoid(f)
    qs_ref[...] = q * _sigmoid(q)
    lf_ref[...] = jnp.log(forget)
    kk_ref[...] = (1.0 - lb) * _sigmoid(-f)
    v_ref[...] = i.astype(BF16)
    gs_ref[...] = g * _sigmoid(g)
    xq_ref[...] = (xq * X_DIM ** -0.5).astype(BF16)


def _pre1(x, g, w, lb, layer, tm):
    n, d = x.shape
    hw = HG_HEADS * HG_DIM
    xw = X_HEADS * X_DIM
    row = lambda wd: pl.BlockSpec((tm, wd), lambda i: (i, 0))
    f32o = jax.ShapeDtypeStruct((n, hw), F32)
    return pl.pallas_call(
        functools.partial(_pre1_kernel, layer=layer),
        grid=(n // tm,),
        in_specs=[row(d), _const_spec(g.shape), _const_spec(w.shape), _const_spec(lb.shape)],
        out_specs=[row(hw), row(hw), row(hw), row(hw), row(hw), row(xw)],
        out_shape=[f32o, f32o, f32o, jax.ShapeDtypeStruct((n, hw), BF16), f32o,
                   jax.ShapeDtypeStruct((n, xw), BF16)],
        compiler_params=_params("arbitrary"),
        name="pre1",
    )(x, g, w, lb)


def _split3(x):
    hi = x.astype(BF16)
    r1 = x - hi.astype(F32)
    mid = r1.astype(BF16)
    lo = (r1 - mid.astype(F32)).astype(BF16)
    return hi, mid, lo


def _gla_kernel(qs_ref, lf_ref, kk_ref, v_ref, gs_ref, s0_ref, gn_ref, mix_ref, sfin_ref,
                st_ref, *, cb):
    blk = pl.program_id(1)
    t = cb * CHUNK
    d = HG_DIM

    @pl.when(blk == 0)
    def _():
        for h in range(HG_HEADS):
            st_ref[h] = s0_ref[0, h].T

    row = lax.broadcasted_iota(jnp.int32, (t, t), 0)
    col = lax.broadcasted_iota(jnp.int32, (t, t), 1)
    causal = jnp.logical_and(row // CHUNK == col // CHUNK, col <= row)
    tri = jnp.where(causal, 1.0, 0.0).astype(BF16)
    gn = gn_ref[...]

    for h in range(HG_HEADS):
        sl = slice(h * d, (h + 1) * d)
        bb = _dot(tri, jnp.concatenate(_split3(lf_ref[:, sl]), axis=1))
        b = (bb[:, :d] + bb[:, d:2 * d]) + bb[:, 2 * d:]
        bcast = lambda r: jnp.concatenate(
            [jnp.broadcast_to(b[c * CHUNK + r:c * CHUNK + r + 1, :], (CHUNK, d)) for c in range(cb)],
            axis=0)
        ref = bcast(CHUNK // 2)
        last = bcast(CHUNK - 1)
        qs = qs_ref[:, sl]
        kk = kk_ref[:, sl]
        v = v_ref[:, sl]
        qe = (qs * jnp.exp(b - ref)).astype(BF16)
        ke = (kk * jnp.exp(ref - b)).astype(BF16)
        a = jnp.where(causal, _dot_nt(qe, ke), 0.0)
        o = _dot(a.astype(BF16), v)
        kh = (kk * jnp.exp(last - b)).astype(BF16)
        qb = (qs * jnp.exp(b)).astype(BF16)
        st = st_ref[h]
        inter = []
        for c in range(cb):
            rows = slice(c * CHUNK, (c + 1) * CHUNK)
            inter.append(_dot_nt(qb[rows], st.astype(BF16)))
            decay = jnp.exp(b[c * CHUNK + CHUNK - 1:c * CHUNK + CHUNK, :])
            st = st * decay + _dot_tn(v[rows], kh[rows])
        st_ref[h] = st
        o = o + (inter[0] if cb == 1 else jnp.concatenate(inter, axis=0))
        mix_ref[:, sl] = (_rms(o, gn) * gs_ref[:, sl]).astype(BF16)

    @pl.when(blk == pl.num_programs(1) - 1)
    def _():
        for h in range(HG_HEADS):
            sfin_ref[0, h] = st_ref[h].T


def _gla(qs, lf, kk, v, gs, s0, gn, nbatch, cb):
    n, hw = qs.shape
    t = cb * CHUNK
    nblk = n // nbatch // t
    row = pl.BlockSpec((t, hw), lambda b, i: (b * nblk + i, 0))
    st = pl.BlockSpec((1, HG_HEADS, HG_DIM, HG_DIM), lambda b, i: (b, 0, 0, 0))
    return pl.pallas_call(
        functools.partial(_gla_kernel, cb=cb),
        grid=(nbatch, nblk),
        in_specs=[row, row, row, row, row, st, _const_spec(gn.shape)],
        out_specs=[row, st],
        out_shape=[jax.ShapeDtypeStruct((n, hw), BF16),
                   jax.ShapeDtypeStruct(s0.shape, F32)],
        scratch_shapes=[pltpu.VMEM((HG_HEADS, HG_DIM, HG_DIM), F32)],
        compiler_params=_params("arbitrary", "arbitrary"),
        name="gla",
    )(qs, lf, kk, v, gs, s0, gn)


def _rope_tables(pos):
    half = MLA_ROPE // 2
    inv = jnp.power(ROPE_THETA, -jnp.arange(half, dtype=F32) / half)
    ang = pos.astype(F32)[:, None] * inv[None, :]
    cos, sin = jnp.cos(ang), jnp.sin(ang)
    z = jnp.zeros_like(cos)
    c = jnp.concatenate([cos, cos, z, z], axis=1)
    s1 = jnp.concatenate([z, sin, z, z], axis=1)
    s2 = jnp.concatenate([-sin, z, z, z], axis=1)
    return c, s1, s2


def _prep_weights(mla_w_in, mla_w_uq, mla_w_uk, mla_w_uv):
    o1 = MLA_Q_RANK + MLA_KV_RANK
    o2 = o1 + MLA_ROPE
    d = mla_w_in.shape[0]
    win = jnp.concatenate([mla_w_in[:, :o1], mla_w_in[:, o2:], mla_w_in[:, o1:o2],
                           jnp.zeros((d, LANES - MLA_ROPE), mla_w_in.dtype)], axis=1)
    wq = mla_w_uq.reshape(MLA_Q_RANK, MLA_HEADS, MLA_NOPE + MLA_ROPE)
    nope = wq[:, :, :MLA_NOPE].reshape(MLA_Q_RANK, MLA_HEADS * MLA_NOPE)
    rope = jnp.pad(wq[:, :, MLA_NOPE:], ((0, 0), (0, 0), (0, LANES - MLA_ROPE)))
    wuq = jnp.concatenate([nope, rope.reshape(MLA_Q_RANK, MLA_HEADS * LANES)], axis=1)
    wuk = mla_w_uk.reshape(MLA_KV_RANK, MLA_HEADS * MLA_NOPE)
    wuv = mla_w_uv.reshape(MLA_KV_RANK, MLA_HEADS * MLA_V)
    return win.astype(BF16), wuq.astype(BF16), wuk.astype(BF16), wuv.astype(BF16)


def kernel(x_prompt, x_sample, cache_mla_latent, cache_mla_krope, cache_hgrn_state, cache_mem_k,
           cache_mem_v, mem_prompt, ln_mix_pre, ln_mix_post, ln_ffn_pre, ln_ffn_post, mem_norm,
           w_mem_kv, mla_w_in, mla_q_norm, mla_kv_norm, mla_w_uq, mla_w_uk, mla_w_uv, mla_w_out,
           hgrn_w_in, hgrn_lb, hgrn_o_norm, hgrn_w_out, w_ffn_up, w_ffn_down):
    bp, tp, d = x_prompt.shape
    bs, ts, _ = x_sample.shape
    depth = ln_mix_pre.shape[0]
    past = cache_mla_latent.shape[2]
    n_mem = mem_prompt.shape[1]
    xw = X_HEADS * X_DIM
    assert depth == 2 and bp == 1 and ts == CHUNK and past % CHUNK == 0
    assert MLA_NOPE == LANES and MLA_V == LANES and HG_DIM == LANES and X_DIM == LANES

    row2 = lambda a: a.reshape(1, -1)
    win0, wuq, wuk, wuv = _prep_weights(mla_w_in[0], mla_w_uq[0], mla_w_uk[0], mla_w_uv[0])
    wout = (mla_w_out[0].astype(BF16), hgrn_w_out[0].astype(BF16))
    wup = w_ffn_up.astype(BF16)
    wdn = w_ffn_down.astype(BF16)
    win1 = hgrn_w_in[0].astype(BF16)

    mem_k_p, mem_v_p = _memkv(mem_prompt, mem_norm, w_mem_kv.astype(BF16))

    def trunk(x, pos, mem_k, mem_v, s0, nbatch, tm, tb, history):
        n = x.shape[0]
        c, s1, s2 = _rope_tables(pos)
        qn, qr, kn, kr, v, lat, krope, xq = _pre0(
            x, row2(ln_mix_pre[0]), win0, row2(mla_q_norm[0]), row2(mla_kv_norm[0]),
            wuq, wuk, wuv, c, s1, s2, tm)
        if history is None:
            mix = _attn_prompt(qn, qr, kn, kr, v, 512)
        else:
            hist_lat, hist_kr = history
            knp, krp, vp = _expand(hist_lat.reshape(-1, MLA_KV_RANK),
                                   hist_kr.reshape(-1, MLA_ROPE), wuk, wuv, 512)
            shp = lambda a: a.reshape(nbatch, past, a.shape[-1])
            mix = _attn_sample(qn, qr, shp(knp), shp(krp), shp(vp), kn, kr, v, ts)
        x = _post(x, mix, xq, mem_k[0].astype(BF16), mem_v[0].astype(BF16), wout[0],
                  row2(ln_mix_post[0]), row2(ln_ffn_pre[0]), wup[0], wdn[0],
                  row2(ln_ffn_post[0]), tm, tb)
        qs, lf, kk, vv, gs, xq = _pre1(x, row2(ln_mix_pre[1]), win1, hgrn_lb, 1, 256)
        mix, s_fin = _gla(qs, lf, kk, vv, gs, s0, row2(hgrn_o_norm[0]), nbatch,
                          4 if n // nbatch >= 4 * CHUNK else 1)
        x = _post(x, mix, xq, mem_k[1].astype(BF16), mem_v[1].astype(BF16), wout[1],
                  row2(ln_mix_post[1]), row2(ln_ffn_pre[1]), wup[1], wdn[1],
                  row2(ln_ffn_post[1]), tm, tb)
        return x, lat, krope, s_fin

    mk_p = mem_k_p.reshape(depth, bp, n_mem, xw)
    mv_p = mem_v_p.reshape(depth, bp, n_mem, xw)
    s0_p = jnp.zeros((bp, HG_HEADS, HG_DIM, HG_DIM), F32)
    y_p, lat_p, kr_p, st_p = trunk(x_prompt.reshape(bp * tp, d), jnp.arange(tp), mk_p, mv_p,
                                   s0_p, bp, 512, 512, None)
    pos_s = jnp.tile(past + jnp.arange(ts), bs)
    mk_s = cache_mem_k.reshape(depth, bs, n_mem, xw)
    mv_s = cache_mem_v.reshape(depth, bs, n_mem, xw)
    y_s, lat_s, kr_s, st_s = trunk(x_sample.reshape(bs * ts, d), pos_s, mk_s, mv_s,
                                   cache_hgrn_state[0], bs, 4 * ts, ts,
                                   (cache_mla_latent[0], cache_mla_krope[0]))

    return (y_p.reshape(bp, tp, d), y_s.reshape(bs, ts, d),
            lat_p.reshape(1, bp, tp, MLA_KV_RANK), kr_p.reshape(1, bp, tp, MLA_ROPE),
            st_p.reshape(1, bp, HG_HEADS, HG_DIM, HG_DIM),
            mem_k_p.reshape(depth, bp, n_mem, X_HEADS, X_DIM),
            mem_v_p.reshape(depth, bp, n_mem, X_HEADS, X_DIM),
            lat_s.reshape(1, bs, ts, MLA_KV_RANK), kr_s.reshape(1, bs, ts, MLA_ROPE),
            st_s.reshape(1, bs, HG_HEADS, HG_DIM, HG_DIM))
```

```python
import functools

import jax
import jax.numpy as jnp
from jax import lax
from jax.experimental import pallas as pl
from jax.experimental.pallas import tpu as pltpu

F32 = jnp.float32
BF16 = jnp.bfloat16

CHUNK = 64
EPS = 1e-6
ROPE_THETA = 10000.0
NEG = -1e30
MLA_HEADS = 8
MLA_NOPE = 128
MLA_ROPE = 64
MLA_V = 128
MLA_Q_RANK = 384
MLA_KV_RANK = 256
HG_HEADS = 8
HG_DIM = 128
X_HEADS = 4
X_DIM = 128
LANES = 128
LOG2E = 1.4426950408889634

VMEM_LIMIT_BYTES = 52 * 1024 * 1024

ATTN_TILE = 512
ATTN_HEADS_PER_STEP = 2


def _dot(a, b):
    return jnp.dot(a, b, preferred_element_type=F32)


def _dot_nt(a, b):
    return lax.dot_general(a, b, (((1,), (1,)), ((), ())), preferred_element_type=F32)


def _dot_tn(a, b):
    return lax.dot_general(a, b, (((0,), (0,)), ((), ())), preferred_element_type=F32)


def _rms(x, g):
    ms = jnp.mean(x * x, axis=-1, keepdims=True)
    return x * lax.rsqrt(ms + EPS) * g


def _sigmoid(x):
    return 1.0 / (1.0 + jnp.exp(-x))


def _params(*sem):
    return pltpu.CompilerParams(dimension_semantics=sem, vmem_limit_bytes=VMEM_LIMIT_BYTES)


def _const_spec(shape):
    nd = len(shape)
    return pl.BlockSpec(shape, lambda *_: (0,) * nd, pipeline_mode=pl.Buffered(1))


def _memkv_kernel(mem_ref, g_ref, w_ref, k_ref, v_ref):
    h = _rms(mem_ref[0], g_ref[0]).astype(BF16)
    kv = _dot(h, w_ref[0])
    half = kv.shape[1] // 2
    k_ref[0, 0] = kv[:, :half]
    v_ref[0, 0] = kv[:, half:]


def _memkv(mem, g, w):
    bp, n_mem, d = mem.shape
    depth = g.shape[0]
    width = w.shape[2] // 2
    out = jax.ShapeDtypeStruct((depth, bp, n_mem, width), F32)
    return pl.pallas_call(
        _memkv_kernel,
        grid=(depth, bp),
        in_specs=[
            pl.BlockSpec((1, n_mem, d), lambda l, b: (b, 0, 0)),
            pl.BlockSpec((1, 1, d), lambda l, b: (l, 0, 0)),
            pl.BlockSpec((1, d, 2 * width), lambda l, b: (l, 0, 0)),
        ],
        out_specs=[
            pl.BlockSpec((1, 1, n_mem, width), lambda l, b: (l, b, 0, 0)),
            pl.BlockSpec((1, 1, n_mem, width), lambda l, b: (l, b, 0, 0)),
        ],
        out_shape=[out, out],
        compiler_params=_params("arbitrary", "arbitrary"),
        name="memkv",
    )(mem, g.reshape(depth, 1, d), w)


def _rope_rot(x, c, s1, s2):
    w = x.shape[1]
    half = MLA_ROPE // 2
    return x * c + pltpu.roll(x, half, 1) * s1 + pltpu.roll(x, w - half, 1) * s2


def _pre0_kernel(*refs, transposed):
    (x_ref, g_ref, win_ref, qg_ref, kvg_ref, wq_ref, wuk_ref, wv_ref,
     c_ref, s1_ref, s2_ref) = refs[:11]
    if transposed:
        ct_ref, st_ref, q_ref, kn_ref, kr_ref, v_ref, lat_ref, krope_ref, xq_ref = refs[11:]
    else:
        qn_ref, qr_ref, kn_ref, kr_ref, v_ref, lat_ref, krope_ref, xq_ref = refs[11:]
    h = _rms(x_ref[...], g_ref[...]).astype(BF16)
    proj = _dot(h, win_ref[...])
    o1 = MLA_Q_RANK
    o2 = o1 + MLA_KV_RANK
    o3 = o2 + X_HEADS * X_DIM
    c_q, c_kv, xq, k_r = proj[:, :o1], proj[:, o1:o2], proj[:, o2:o3], proj[:, o3:]

    scale = (MLA_NOPE + MLA_ROPE) ** -0.5 * LOG2E
    qc = _rms(c_q, qg_ref[...]).astype(BF16)
    c, s1, s2 = c_ref[...], s1_ref[...], s2_ref[...]
    lat = _rms(c_kv, kvg_ref[...])
    lat_ref[...] = lat
    latb = lat.astype(BF16)
    kn_ref[...] = _dot(latb, wuk_ref[...]).astype(BF16)
    if transposed:
        qt = _dot_nt(wq_ref[...], qc) * scale
        cos, sin = ct_ref[...], st_ref[...]
        half = MLA_ROPE // 2
        slabs = []
        for hd in range(MLA_HEADS):
            r0 = hd * 2 * LANES + MLA_NOPE
            x1, x2 = qt[r0:r0 + half], qt[r0 + half:r0 + 2 * half]
            slabs += [qt[hd * 2 * LANES:r0], x1 * cos - x2 * sin, x2 * cos + x1 * sin,
                      qt[r0 + 2 * half:(hd + 1) * 2 * LANES]]
        q_ref[...] = jnp.concatenate(slabs, axis=0).astype(BF16)
        v_ref[0] = _dot_nt(wv_ref[...], latb).astype(BF16)
    else:
        q = _dot(qc, wq_ref[...]) * scale
        nope_w = MLA_HEADS * MLA_NOPE
        qn_ref[...] = q[:, :nope_w].astype(BF16)
        tile = lambda t: jnp.concatenate([t] * MLA_HEADS, axis=1)
        qr_ref[...] = _rope_rot(q[:, nope_w:], tile(c), tile(s1), tile(s2)).astype(BF16)
        v_ref[...] = _dot(latb, wv_ref[...]).astype(BF16)

    kr = _rope_rot(k_r, c, s1, s2)
    krope_ref[...] = kr[:, :MLA_ROPE]
    kr_ref[...] = kr.astype(BF16)
    xq_ref[...] = (xq * (X_DIM ** -0.5 * LOG2E)).astype(BF16)


def _pre0(x, g, win, qg, kvg, wq, wuk, wv, tables, tm, transposed):
    n, d = x.shape
    hw = MLA_HEADS * LANES
    xw = X_HEADS * X_DIM
    row = lambda w: pl.BlockSpec((tm, w), lambda i: (i, 0))
    col = lambda r: pl.BlockSpec((r, tm), lambda i: (0, i))
    sds = jax.ShapeDtypeStruct
    in_specs = [row(d), _const_spec(g.shape), _const_spec(win.shape), _const_spec(qg.shape),
                _const_spec(kvg.shape), _const_spec(wq.shape), _const_spec(wuk.shape),
                _const_spec(wv.shape), row(LANES), row(LANES), row(LANES)]
    tail_specs = [row(hw), row(LANES)]
    tail_shapes = [sds((n, hw), BF16),
                   sds((n, LANES), BF16)]
    common_specs = [row(MLA_KV_RANK), row(MLA_ROPE), row(xw)]
    common_shapes = [sds((n, MLA_KV_RANK), F32),
                     sds((n, MLA_ROPE), F32),
                     sds((n, xw), BF16)]
    if transposed:
        half = MLA_ROPE // 2
        in_specs += [col(half), col(half)]
        out_specs = [col(2 * hw)] + tail_specs + [pl.BlockSpec((1, hw, tm), lambda i: (i, 0, 0))]
        out_shape = [sds((2 * hw, n), BF16)] + tail_shapes + [sds((n // tm, hw, tm), BF16)]
    else:
        out_specs = [row(hw), row(hw)] + tail_specs + [row(hw)]
        out_shape = [sds((n, hw), BF16), sds((n, hw), BF16)] + tail_shapes + [sds((n, hw), BF16)]
    return pl.pallas_call(
        functools.partial(_pre0_kernel, transposed=transposed),
        grid=(n // tm,),
        in_specs=in_specs,
        out_specs=out_specs + common_specs,
        out_shape=out_shape + common_shapes,
        compiler_params=_params("arbitrary"),
        name="pre0",
    )(x, g, win, qg, kvg, wq, wuk, wv, *tables)


def _expand_kernel(lat_ref, kr_ref, wuk_ref, wuv_ref, kn_ref, krp_ref, v_ref):
    latb = lat_ref[...].astype(BF16)
    kn_ref[...] = _dot(latb, wuk_ref[...]).astype(BF16)
    v_ref[...] = _dot(latb, wuv_ref[...]).astype(BF16)
    kr = kr_ref[...].astype(BF16)
    krp_ref[...] = jnp.concatenate([kr, jnp.zeros_like(kr)], axis=1)


def _expand(lat, kr, wuk, wuv, tm):
    n = lat.shape[0]
    hw = MLA_HEADS * LANES
    row = lambda w: pl.BlockSpec((tm, w), lambda i: (i, 0))
    return pl.pallas_call(
        _expand_kernel,
        grid=(n // tm,),
        in_specs=[row(MLA_KV_RANK), row(MLA_ROPE), _const_spec(wuk.shape), _const_spec(wuv.shape)],
        out_specs=[row(hw), row(LANES), row(hw)],
        out_shape=[jax.ShapeDtypeStruct((n, hw), BF16),
                   jax.ShapeDtypeStruct((n, LANES), BF16),
                   jax.ShapeDtypeStruct((n, hw), BF16)],
        compiler_params=_params("arbitrary"),
        name="expand_cache",
    )(lat, kr, wuk, wuv)


def _attn_p_kernel(q_ref, kn_ref, kr_ref, v_ref, o_ref, m_ref, l_ref, acc_ref, sa_ref, sb_ref,
                   *, t, hps):
    qi = pl.program_id(1)
    m_ref[...] = jnp.full(m_ref.shape, NEG, F32)
    l_ref[...] = jnp.zeros(l_ref.shape, F32)
    acc_ref[...] = jnp.zeros(acc_ref.shape, F32)

    def scores(j, s_ref):
        kr = kr_ref[j]
        for hh in range(hps):
            k_blk = jnp.concatenate([kn_ref[j, :, hh * LANES:(hh + 1) * LANES], kr], axis=1)
            s_ref[hh] = _dot(k_blk, q_ref[hh * 2 * LANES:(hh + 1) * 2 * LANES, :])

    def update(j, s_ref, masked):
        if masked:
            keep = (lax.broadcasted_iota(jnp.int32, (t, t), 0) // CHUNK
                    <= lax.broadcasted_iota(jnp.int32, (t, t), 1) // CHUNK)
        for hh in range(hps):
            s = s_ref[hh]
            if masked:
                s = jnp.where(keep, s, NEG)
            m_prev = m_ref[hh]
            m_new = jnp.maximum(m_prev, jnp.max(s, axis=0, keepdims=True))
            alpha = jnp.exp2(m_prev - m_new)
            p = jnp.exp2(s - m_new)
            l_ref[hh] = alpha * l_ref[hh] + jnp.sum(p, axis=0, keepdims=True)
            pv = _dot(v_ref[j, hh * LANES:(hh + 1) * LANES, :], p.astype(BF16))
            acc_ref[hh] = alpha * acc_ref[hh] + pv
            m_ref[hh] = m_new

    scores(0, sa_ref)

    def pair(i, _):
        scores(2 * i + 1, sb_ref)
        update(2 * i, sa_ref, False)
        scores(2 * i + 2, sa_ref)
        update(2 * i + 1, sb_ref, False)
        return 0

    lax.fori_loop(0, qi // 2, pair, 0)

    @pl.when(qi % 2 == 0)
    def _():
        update(qi, sa_ref, True)

    @pl.when(qi % 2 == 1)
    def _():
        scores(qi, sb_ref)
        update(qi - 1, sa_ref, False)
        update(qi, sb_ref, True)

    for hh in range(hps):
        o = acc_ref[hh] / l_ref[hh]
        o_ref[:, hh * LANES:(hh + 1) * LANES] = o.T.astype(BF16)


def _attn_prompt(qt, kn, kr, vt3, t, hps):
    n, hw = kn.shape
    nb = n // t
    kn3 = kn.reshape(nb, t, hw)
    kr3 = kr.reshape(nb, t, LANES)
    return pl.pallas_call(
        functools.partial(_attn_p_kernel, t=t, hps=hps),
        grid=(MLA_HEADS // hps, nb),
        in_specs=[
            pl.BlockSpec((hps * 2 * LANES, t), lambda h, i: (h, i)),
            pl.BlockSpec((nb, t, hps * LANES), lambda h, i: (0, 0, h), pipeline_mode=pl.Buffered(1)),
            pl.BlockSpec((nb, t, LANES), lambda h, i: (0, 0, 0), pipeline_mode=pl.Buffered(1)),
            pl.BlockSpec((nb, hps * LANES, t), lambda h, i: (0, h, 0), pipeline_mode=pl.Buffered(1)),
        ],
        out_specs=pl.BlockSpec((t, hps * LANES), lambda h, i: (i, h)),
        out_shape=jax.ShapeDtypeStruct((n, hw), BF16),
        scratch_shapes=[pltpu.VMEM((hps, 1, t), F32), pltpu.VMEM((hps, 1, t), F32),
                        pltpu.VMEM((hps, LANES, t), F32),
                        pltpu.VMEM((hps, t, t), F32), pltpu.VMEM((hps, t, t), F32)],
        compiler_params=_params("arbitrary", "arbitrary"),
        name="attn_prompt",
    )(qt, kn3, kr3, vt3)


def _attn_s_kernel(qn_ref, qr_ref, knp_ref, krp_ref, vp_ref, kn_ref, kr_ref, v_ref, o_ref,
                   *, past, ts):
    q_chunk = (past + lax.broadcasted_iota(jnp.int32, (ts, 1), 0)) // CHUNK
    keep_p = (lax.broadcasted_iota(jnp.int32, (ts, past), 1) // CHUNK) <= q_chunk
    keep_n = ((past + lax.broadcasted_iota(jnp.int32, (ts, ts), 1)) // CHUNK) <= q_chunk
    krp = krp_ref[0]
    kr = kr_ref[...]
    for h in range(MLA_HEADS):
        sl = slice(h * LANES, (h + 1) * LANES)
        q_h = jnp.concatenate([qn_ref[:, sl], qr_ref[:, sl]], axis=1)
        s_p = _dot_nt(q_h, jnp.concatenate([knp_ref[0, :, sl], krp], axis=1))
        s_n = _dot_nt(q_h, jnp.concatenate([kn_ref[:, sl], kr], axis=1))
        s_p = jnp.where(keep_p, s_p, NEG)
        s_n = jnp.where(keep_n, s_n, NEG)
        m = jnp.maximum(jnp.max(s_p, axis=1, keepdims=True), jnp.max(s_n, axis=1, keepdims=True))
        p_p = jnp.exp2(s_p - m)
        p_n = jnp.exp2(s_n - m)
        l = jnp.sum(p_p, axis=1, keepdims=True) + jnp.sum(p_n, axis=1, keepdims=True)
        o = _dot(p_p.astype(BF16), vp_ref[0, :, sl]) + _dot(p_n.astype(BF16), v_ref[:, sl])
        o_ref[:, sl] = (o / l).astype(BF16)


def _attn_sample(qn, qr, knp, krp, vp, kn, kr, v, ts):
    n, hw = qn.shape
    nb, past, _ = knp.shape
    new = lambda w: pl.BlockSpec((ts, w), lambda b: (b, 0))
    old = lambda w: pl.BlockSpec((1, past, w), lambda b: (b, 0, 0))
    return pl.pallas_call(
        functools.partial(_attn_s_kernel, past=past, ts=ts),
        grid=(nb,),
        in_specs=[new(hw), new(hw), old(hw), old(LANES), old(hw), new(hw), new(LANES), new(hw)],
        out_specs=new(hw),
        out_shape=jax.ShapeDtypeStruct((n, hw), BF16),
        compiler_params=_params("arbitrary"),
        name="attn_sample",
    )(qn, qr, knp, krp, vp, kn, kr, v)


def _post_kernel(x_ref, mix_ref, xq_ref, mk_ref, mv_ref, wo_ref, g1_ref, g2_ref, wup_ref,
                 wdn_ref, g3_ref, y_ref, *, nb, tb):
    cross = []
    for b in range(nb):
        rows = slice(b * tb, (b + 1) * tb)
        heads = []
        for h in range(X_HEADS):
            sl = slice(h * X_DIM, (h + 1) * X_DIM)
            s = _dot_nt(xq_ref[rows, sl], mk_ref[b, :, sl])
            p = jnp.exp2(s - jnp.max(s, axis=1, keepdims=True))
            l = jnp.sum(p, axis=1, keepdims=True)
            heads.append((_dot(p.astype(BF16), mv_ref[b, :, sl]) / l).astype(BF16))
        cross.append(jnp.concatenate(heads, axis=1))
    cross = cross[0] if nb == 1 else jnp.concatenate(cross, axis=0)
    cat = jnp.concatenate([mix_ref[...], cross], axis=1)
    x1 = x_ref[...] + _rms(_dot(cat, wo_ref[...]), g1_ref[...])

    hb = _rms(x1, g2_ref[...]).astype(BF16)
    d_ff = wup_ref.shape[1]
    step = 1024
    acc = None
    for c in range(d_ff // step):
        up = jnp.maximum(_dot(hb, wup_ref[:, c * step:(c + 1) * step]), 0.0)
        dn = _dot((up * up).astype(BF16), wdn_ref[c * step:(c + 1) * step, :])
        acc = dn if acc is None else acc + dn
    y_ref[...] = x1 + _rms(acc, g3_ref[...])


def _post(x, mix, xq, mk, mv, wo, g1, g2, wup, wdn, g3, tm, tb):
    n, d = x.shape
    nb = tm // tb
    n_mem, xw = mk.shape[1], mk.shape[2]
    row = lambda w: pl.BlockSpec((tm, w), lambda i: (i, 0))
    if mk.shape[0] * tb == n:
        mem = pl.BlockSpec((nb, n_mem, xw), lambda i: (i, 0, 0))
    else:
        assert mk.shape[0] == 1 and nb == 1
        mem = pl.BlockSpec((1, n_mem, xw), lambda i: (0, 0, 0))
    return pl.pallas_call(
        functools.partial(_post_kernel, nb=nb, tb=tb),
        grid=(n // tm,),
        in_specs=[row(d), row(mix.shape[1]), row(xw), mem, mem, _const_spec(wo.shape),
                  _const_spec(g1.shape), _const_spec(g2.shape), _const_spec(wup.shape),
                  _const_spec(wdn.shape), _const_spec(g3.shape)],
        out_specs=row(d),
        out_shape=jax.ShapeDtypeStruct((n, d), F32),
        compiler_params=_params("arbitrary"),
        name="post",
    )(x, mix, xq, mk, mv, wo, g1, g2, wup, wdn, g3)


def _pre1_kernel(x_ref, g_ref, w_ref, lb_ref,
                 qs_ref, lf_ref, kk_ref, v_ref, gs_ref, xq_ref, *, layer):
    hb = _rms(x_ref[...], g_ref[...]).astype(BF16)
    proj = _dot(hb, w_ref[...])
    w = HG_HEADS * HG_DIM
    q, f, i, g, xq = (proj[:, :w], proj[:, w:2 * w], proj[:, 2 * w:3 * w],
                      proj[:, 3 * w:4 * w], proj[:, 4 * w:])
    lb_all = lb_ref[...]
    e = jnp.exp(lb_all - jnp.max(lb_all, axis=0, keepdims=True))
    soft = e / jnp.sum(e, axis=0, keepdims=True)
    lb = jnp.sum(soft[:layer + 1], axis=0, keepdims=True) - soft[0:1]
    forget = lb + (1.0 - lb) * _sigmoid(f)
    qs_ref[...] = q * _sigmoid(q)
    lf_ref[...] = jnp.log(forget)
    kk_ref[...] = (1.0 - lb) * _sigmoid(-f)
    v_ref[...] = i.astype(BF16)
    gs_ref[...] = g * _sigmoid(g)
    xq_ref[...] = (xq * (X_DIM ** -0.5 * LOG2E)).astype(BF16)


def _pre1(x, g, w, lb, layer, tm):
    n, d = x.shape
    hw = HG_HEADS * HG_DIM
    xw = X_HEADS * X_DIM
    row = lambda wd: pl.BlockSpec((tm, wd), lambda i: (i, 0))
    f32o = jax.ShapeDtypeStruct((n, hw), F32)
    return pl.pallas_call(
        functools.partial(_pre1_kernel, layer=layer),
        grid=(n // tm,),
        in_specs=[row(d), _const_spec(g.shape), _const_spec(w.shape), _const_spec(lb.shape)],
        out_specs=[row(hw), row(hw), row(hw), row(hw), row(hw), row(xw)],
        out_shape=[f32o, f32o, f32o, jax.ShapeDtypeStruct((n, hw), BF16), f32o,
                   jax.ShapeDtypeStruct((n, xw), BF16)],
        compiler_params=_params("arbitrary"),
        name="pre1",
    )(x, g, w, lb)


def _split3(x):
    hi = x.astype(BF16)
    r1 = x - hi.astype(F32)
    mid = r1.astype(BF16)
    lo = (r1 - mid.astype(F32)).astype(BF16)
    return hi, mid, lo


def _gla_kernel(qs_ref, lf_ref, kk_ref, v_ref, gs_ref, s0_ref, gn_ref, mix_ref, sfin_ref,
                st_ref, *, cb):
    blk = pl.program_id(1)
    t = cb * CHUNK
    d = HG_DIM

    @pl.when(blk == 0)
    def _():
        for h in range(HG_HEADS):
            st_ref[h] = s0_ref[0, h].T

    row = lax.broadcasted_iota(jnp.int32, (t, t), 0)
    col = lax.broadcasted_iota(jnp.int32, (t, t), 1)
    causal = jnp.logical_and(row // CHUNK == col // CHUNK, col <= row)
    tri = jnp.where(causal, 1.0, 0.0).astype(BF16)
    gn = gn_ref[...]

    for h in range(HG_HEADS):
        sl = slice(h * d, (h + 1) * d)
        bb = _dot(tri, jnp.concatenate(_split3(lf_ref[:, sl]), axis=1))
        b = (bb[:, :d] + bb[:, d:2 * d]) + bb[:, 2 * d:]
        bcast = lambda r: jnp.concatenate(
            [jnp.broadcast_to(b[c * CHUNK + r:c * CHUNK + r + 1, :], (CHUNK, d)) for c in range(cb)],
            axis=0)
        ref = bcast(CHUNK // 2)
        last = bcast(CHUNK - 1)
        qs = qs_ref[:, sl]
        kk = kk_ref[:, sl]
        v = v_ref[:, sl]
        qe = (qs * jnp.exp(b - ref)).astype(BF16)
        ke = (kk * jnp.exp(ref - b)).astype(BF16)
        a = jnp.where(causal, _dot_nt(qe, ke), 0.0)
        o = _dot(a.astype(BF16), v)
        kh = (kk * jnp.exp(last - b)).astype(BF16)
        qb = (qs * jnp.exp(b)).astype(BF16)
        st = st_ref[h]
        inter = []
        for c in range(cb):
            rows = slice(c * CHUNK, (c + 1) * CHUNK)
            inter.append(_dot_nt(qb[rows], st.astype(BF16)))
            decay = jnp.exp(b[c * CHUNK + CHUNK - 1:c * CHUNK + CHUNK, :])
            st = st * decay + _dot_tn(v[rows], kh[rows])
        st_ref[h] = st
        o = o + (inter[0] if cb == 1 else jnp.concatenate(inter, axis=0))
        mix_ref[:, sl] = (_rms(o, gn) * gs_ref[:, sl]).astype(BF16)

    @pl.when(blk == pl.num_programs(1) - 1)
    def _():
        for h in range(HG_HEADS):
            sfin_ref[0, h] = st_ref[h].T


def _gla(qs, lf, kk, v, gs, s0, gn, nbatch, cb):
    n, hw = qs.shape
    t = cb * CHUNK
    nblk = n // nbatch // t
    row = pl.BlockSpec((t, hw), lambda b, i: (b * nblk + i, 0))
    st = pl.BlockSpec((1, HG_HEADS, HG_DIM, HG_DIM), lambda b, i: (b, 0, 0, 0))
    return pl.pallas_call(
        functools.partial(_gla_kernel, cb=cb),
        grid=(nbatch, nblk),
        in_specs=[row, row, row, row, row, st, _const_spec(gn.shape)],
        out_specs=[row, st],
        out_shape=[jax.ShapeDtypeStruct((n, hw), BF16),
                   jax.ShapeDtypeStruct(s0.shape, F32)],
        scratch_shapes=[pltpu.VMEM((HG_HEADS, HG_DIM, HG_DIM), F32)],
        compiler_params=_params("arbitrary", "arbitrary"),
        name="gla",
    )(qs, lf, kk, v, gs, s0, gn)


def _rope_tables(pos):
    half = MLA_ROPE // 2
    inv = jnp.power(ROPE_THETA, -jnp.arange(half, dtype=F32) / half)
    ang = pos.astype(F32)[:, None] * inv[None, :]
    cos, sin = jnp.cos(ang), jnp.sin(ang)
    z = jnp.zeros_like(cos)
    c = jnp.concatenate([cos, cos, z, z], axis=1)
    s1 = jnp.concatenate([z, sin, z, z], axis=1)
    s2 = jnp.concatenate([-sin, z, z, z], axis=1)
    return (c, s1, s2), (cos.T, sin.T)


def _prep_weights(mla_w_in, mla_w_uq, mla_w_uk, mla_w_uv):
    o1 = MLA_Q_RANK + MLA_KV_RANK
    o2 = o1 + MLA_ROPE
    d = mla_w_in.shape[0]
    win = jnp.concatenate([mla_w_in[:, :o1], mla_w_in[:, o2:], mla_w_in[:, o1:o2],
                           jnp.zeros((d, LANES - MLA_ROPE), mla_w_in.dtype)], axis=1)
    wq = mla_w_uq.reshape(MLA_Q_RANK, MLA_HEADS, MLA_NOPE + MLA_ROPE)
    nope = wq[:, :, :MLA_NOPE].reshape(MLA_Q_RANK, MLA_HEADS * MLA_NOPE)
    rope = jnp.pad(wq[:, :, MLA_NOPE:], ((0, 0), (0, 0), (0, LANES - MLA_ROPE)))
    wuq = jnp.concatenate([nope, rope.reshape(MLA_Q_RANK, MLA_HEADS * LANES)], axis=1)
    wuk = mla_w_uk.reshape(MLA_KV_RANK, MLA_HEADS * MLA_NOPE)
    wuv = mla_w_uv.reshape(MLA_KV_RANK, MLA_HEADS * MLA_V)
    wq_t = jnp.pad(wq, ((0, 0), (0, 0), (0, LANES - MLA_ROPE))).reshape(MLA_Q_RANK, -1).T
    bf = lambda a: a.astype(BF16)
    return bf(win), bf(wuq), bf(wuk), bf(wuv), bf(wq_t), bf(wuv.T)


def kernel(x_prompt, x_sample, cache_mla_latent, cache_mla_krope, cache_hgrn_state, cache_mem_k,
           cache_mem_v, mem_prompt, ln_mix_pre, ln_mix_post, ln_ffn_pre, ln_ffn_post, mem_norm,
           w_mem_kv, mla_w_in, mla_q_norm, mla_kv_norm, mla_w_uq, mla_w_uk, mla_w_uv, mla_w_out,
           hgrn_w_in, hgrn_lb, hgrn_o_norm, hgrn_w_out, w_ffn_up, w_ffn_down):
    bp, tp, d = x_prompt.shape
    bs, ts, _ = x_sample.shape
    depth = ln_mix_pre.shape[0]
    past = cache_mla_latent.shape[2]
    n_mem = mem_prompt.shape[1]
    xw = X_HEADS * X_DIM
    assert depth == 2 and bp == 1 and ts == CHUNK and past % CHUNK == 0
    assert MLA_NOPE == LANES and MLA_V == LANES and HG_DIM == LANES and X_DIM == LANES

    row2 = lambda a: a.reshape(1, -1)
    win0, wuq, wuk, wuv, wuq_t, wuv_t = _prep_weights(mla_w_in[0], mla_w_uq[0], mla_w_uk[0],
                                                      mla_w_uv[0])
    wout = (mla_w_out[0].astype(BF16), hgrn_w_out[0].astype(BF16))
    wup = w_ffn_up.astype(BF16)
    wdn = w_ffn_down.astype(BF16)
    win1 = hgrn_w_in[0].astype(BF16)

    mem_k_p, mem_v_p = _memkv(mem_prompt, mem_norm, w_mem_kv.astype(BF16))

    def trunk(x, pos, mem_k, mem_v, s0, nbatch, tm, tb, history):
        n = x.shape[0]
        tok_tables, lane_tables = _rope_tables(pos)
        pre0_args = (x, row2(ln_mix_pre[0]), win0, row2(mla_q_norm[0]), row2(mla_kv_norm[0]))
        if history is None:
            qt, kn, kr, vt3, lat, krope, xq = _pre0(
                *pre0_args, wuq_t, wuk, wuv_t, tok_tables + lane_tables, ATTN_TILE, True)
            mix = _attn_prompt(qt, kn, kr, vt3, ATTN_TILE, ATTN_HEADS_PER_STEP)
        else:
            qn, qr, kn, kr, v, lat, krope, xq = _pre0(
                *pre0_args, wuq, wuk, wuv, tok_tables, tm, False)
            hist_lat, hist_kr = history
            knp, krp, vp = _expand(hist_lat.reshape(-1, MLA_KV_RANK),
                                   hist_kr.reshape(-1, MLA_ROPE), wuk, wuv, 512)
            shp = lambda a: a.reshape(nbatch, past, a.shape[-1])
            mix = _attn_sample(qn, qr, shp(knp), shp(krp), shp(vp), kn, kr, v, ts)
        x = _post(x, mix, xq, mem_k[0].astype(BF16), mem_v[0].astype(BF16), wout[0],
                  row2(ln_mix_post[0]), row2(ln_ffn_pre[0]), wup[0], wdn[0],
                  row2(ln_ffn_post[0]), tm, tb)
        qs, lf, kk, vv, gs, xq = _pre1(x, row2(ln_mix_pre[1]), win1, hgrn_lb, 1, 256)
        mix, s_fin = _gla(qs, lf, kk, vv, gs, s0, row2(hgrn_o_norm[0]), nbatch,
                          4 if n // nbatch >= 4 * CHUNK else 1)
        x = _post(x, mix, xq, mem_k[1].astype(BF16), mem_v[1].astype(BF16), wout[1],
                  row2(ln_mix_post[1]), row2(ln_ffn_pre[1]), wup[1], wdn[1],
                  row2(ln_ffn_post[1]), tm, tb)
        return x, lat, krope, s_fin

    mk_p = mem_k_p.reshape(depth, bp, n_mem, xw)
    mv_p = mem_v_p.reshape(depth, bp, n_mem, xw)
    s0_p = jnp.zeros((bp, HG_HEADS, HG_DIM, HG_DIM), F32)
    y_p, lat_p, kr_p, st_p = trunk(x_prompt.reshape(bp * tp, d), jnp.arange(tp), mk_p, mv_p,
                                   s0_p, bp, 512, 512, None)
    pos_s = jnp.tile(past + jnp.arange(ts), bs)
    mk_s = cache_mem_k.reshape(depth, bs, n_mem, xw)
    mv_s = cache_mem_v.reshape(depth, bs, n_mem, xw)
    y_s, lat_s, kr_s, st_s = trunk(x_sample.reshape(bs * ts, d), pos_s, mk_s, mv_s,
                                   cache_hgrn_state[0], bs, 4 * ts, ts,
                                   (cache_mla_latent[0], cache_mla_krope[0]))

    return (y_p.reshape(bp, tp, d), y_s.reshape(bs, ts, d),
            lat_p.reshape(1, bp, tp, MLA_KV_RANK), kr_p.reshape(1, bp, tp, MLA_ROPE),
            st_p.reshape(1, bp, HG_HEADS, HG_DIM, HG_DIM),
            mem_k_p.reshape(depth, bp, n_mem, X_HEADS, X_DIM),
            mem_v_p.reshape(depth, bp, n_mem, X_HEADS, X_DIM),
            lat_s.reshape(1, bs, ts, MLA_KV_RANK), kr_s.reshape(1, bs, ts, MLA_ROPE),
            st_s.reshape(1, bs, HG_HEADS, HG_DIM, HG_DIM))
```

```python
import functools

import jax
import jax.numpy as jnp
from jax import lax
from jax.experimental import pallas as pl
from jax.experimental.pallas import tpu as pltpu

F32 = jnp.float32
BF16 = jnp.bfloat16

CHUNK = 64
EPS = 1e-6
ROPE_THETA = 10000.0
NEG = -1e30
MLA_HEADS = 8
MLA_NOPE = 128
MLA_ROPE = 64
MLA_V = 128
MLA_Q_RANK = 384
MLA_KV_RANK = 256
HG_HEADS = 8
HG_DIM = 128
X_HEADS = 4
X_DIM = 128
LANES = 128
LOG2E = 1.4426950408889634

VMEM_LIMIT_BYTES = 52 * 1024 * 1024

ATTN_TILE = 512
ATTN_HEADS_PER_STEP = 2
SUM_ROWS = 16


def _dot(a, b):
    return jnp.dot(a, b, preferred_element_type=F32)


def _dot_nt(a, b):
    return lax.dot_general(a, b, (((1,), (1,)), ((), ())), preferred_element_type=F32)


def _dot_tn(a, b):
    return lax.dot_general(a, b, (((0,), (0,)), ((), ())), preferred_element_type=F32)


def _rms(x, g):
    ms = jnp.mean(x * x, axis=-1, keepdims=True)
    return x * lax.rsqrt(ms + EPS) * g


def _sigmoid(x):
    return 1.0 / (1.0 + jnp.exp(-x))


def _params(*sem):
    return pltpu.CompilerParams(dimension_semantics=sem, vmem_limit_bytes=VMEM_LIMIT_BYTES)


def _const_spec(shape):
    nd = len(shape)
    return pl.BlockSpec(shape, lambda *_: (0,) * nd, pipeline_mode=pl.Buffered(1))


def _memkv_kernel(mem_ref, g_ref, w_ref, k_ref, v_ref):
    h = _rms(mem_ref[0], g_ref[0]).astype(BF16)
    kv = _dot(h, w_ref[0])
    half = kv.shape[1] // 2
    k_ref[0, 0] = kv[:, :half]
    v_ref[0, 0] = kv[:, half:]


def _memkv(mem, g, w):
    bp, n_mem, d = mem.shape
    depth = g.shape[0]
    width = w.shape[2] // 2
    out = jax.ShapeDtypeStruct((depth, bp, n_mem, width), F32)
    return pl.pallas_call(
        _memkv_kernel,
        grid=(depth, bp),
        in_specs=[
            pl.BlockSpec((1, n_mem, d), lambda l, b: (b, 0, 0)),
            pl.BlockSpec((1, 1, d), lambda l, b: (l, 0, 0)),
            pl.BlockSpec((1, d, 2 * width), lambda l, b: (l, 0, 0)),
        ],
        out_specs=[
            pl.BlockSpec((1, 1, n_mem, width), lambda l, b: (l, b, 0, 0)),
            pl.BlockSpec((1, 1, n_mem, width), lambda l, b: (l, b, 0, 0)),
        ],
        out_shape=[out, out],
        compiler_params=_params("arbitrary", "arbitrary"),
        name="memkv",
    )(mem, g.reshape(depth, 1, d), w)


def _rope_rot(x, c, s1, s2):
    w = x.shape[1]
    half = MLA_ROPE // 2
    return x * c + pltpu.roll(x, half, 1) * s1 + pltpu.roll(x, w - half, 1) * s2


def _pre0_kernel(*refs, expand):
    x_ref, g_ref, win_ref, qg_ref, kvg_ref, wq_ref, c_ref, s1_ref, s2_ref = refs[:9]
    if expand:
        (wuk_ref, wv_ref, ct_ref, st_ref,
         q_ref, kn_ref, kr_ref, v_ref, lat_ref, krope_ref, xq_ref) = refs[9:]
    else:
        qn_ref, qr_ref, lat_ref, krope_ref, xq_ref = refs[9:]
    h = _rms(x_ref[...], g_ref[...]).astype(BF16)
    proj = _dot(h, win_ref[...])
    o1 = MLA_Q_RANK
    o2 = o1 + MLA_KV_RANK
    o3 = o2 + X_HEADS * X_DIM
    c_q, c_kv, xq, k_r = proj[:, :o1], proj[:, o1:o2], proj[:, o2:o3], proj[:, o3:]

    scale = (MLA_NOPE + MLA_ROPE) ** -0.5 * LOG2E
    qc = _rms(c_q, qg_ref[...]).astype(BF16)
    c, s1, s2 = c_ref[...], s1_ref[...], s2_ref[...]
    lat = _rms(c_kv, kvg_ref[...])
    lat_ref[...] = lat
    kr = _rope_rot(k_r, c, s1, s2)
    krope_ref[...] = kr[:, :MLA_ROPE]
    xq_ref[...] = (xq * (X_DIM ** -0.5 * LOG2E)).astype(BF16)
    if expand:
        latb = lat.astype(BF16)
        kn_ref[...] = _dot(latb, wuk_ref[...]).astype(BF16)
        kr_ref[...] = kr.astype(BF16)
        qt = _dot_nt(wq_ref[...], qc) * scale
        cos, sin = ct_ref[...], st_ref[...]
        half = MLA_ROPE // 2
        slabs = []
        for hd in range(MLA_HEADS):
            r0 = hd * 2 * LANES + MLA_NOPE
            x1, x2 = qt[r0:r0 + half], qt[r0 + half:r0 + 2 * half]
            slabs += [qt[hd * 2 * LANES:r0], x1 * cos - x2 * sin, x2 * cos + x1 * sin,
                      qt[r0 + 2 * half:(hd + 1) * 2 * LANES]]
        q_ref[...] = jnp.concatenate(slabs, axis=0).astype(BF16)
        v_ref[0] = _dot_nt(wv_ref[...], latb).astype(BF16)
    else:
        q = _dot(qc, wq_ref[...]) * scale
        nope_w = MLA_HEADS * MLA_NOPE
        qn_ref[...] = q[:, :nope_w].astype(BF16)
        tile = lambda t: jnp.concatenate([t] * MLA_HEADS, axis=1)
        qr_ref[...] = _rope_rot(q[:, nope_w:], tile(c), tile(s1), tile(s2)).astype(BF16)


def _pre0(x, g, win, qg, kvg, wq, tables, tm, expand_weights=None):
    n, d = x.shape
    hw = MLA_HEADS * LANES
    xw = X_HEADS * X_DIM
    expand = expand_weights is not None
    row = lambda w: pl.BlockSpec((tm, w), lambda i: (i, 0))
    col = lambda r: pl.BlockSpec((r, tm), lambda i: (0, i))
    sds = jax.ShapeDtypeStruct
    in_specs = [row(d), _const_spec(g.shape), _const_spec(win.shape), _const_spec(qg.shape),
                _const_spec(kvg.shape), _const_spec(wq.shape), row(LANES), row(LANES), row(LANES)]
    common_specs = [row(MLA_KV_RANK), row(MLA_ROPE), row(xw)]
    common_shapes = [sds((n, MLA_KV_RANK), F32),
                     sds((n, MLA_ROPE), F32),
                     sds((n, xw), BF16)]
    if expand:
        half = MLA_ROPE // 2
        in_specs += [_const_spec(w.shape) for w in expand_weights] + [col(half), col(half)]
        out_specs = [col(2 * hw), row(hw), row(LANES), pl.BlockSpec((1, hw, tm), lambda i: (i, 0, 0))]
        out_shape = [sds((2 * hw, n), BF16),
                     sds((n, hw), BF16),
                     sds((n, LANES), BF16),
                     sds((n // tm, hw, tm), BF16)]
        operands = (x, g, win, qg, kvg, wq) + tuple(tables[:3]) + tuple(expand_weights) + tuple(tables[3:])
    else:
        out_specs = [row(hw), row(hw)]
        out_shape = [sds((n, hw), BF16),
                     sds((n, hw), BF16)]
        operands = (x, g, win, qg, kvg, wq) + tuple(tables)
    return pl.pallas_call(
        functools.partial(_pre0_kernel, expand=expand),
        grid=(n // tm,),
        in_specs=in_specs,
        out_specs=out_specs + common_specs,
        out_shape=out_shape + common_shapes,
        compiler_params=_params("arbitrary"),
        name="pre0",
    )(*operands)


def _attn_p_kernel(q_ref, kn_ref, kr_ref, v_ref, o_ref, m_ref, acc_ref, sa_ref, sb_ref,
                   mxa_ref, mxb_ref, *, t, hps):
    qi = pl.program_id(1)
    m_ref[...] = jnp.full(m_ref.shape, NEG, F32)
    acc_ref[...] = jnp.zeros(acc_ref.shape, F32)
    ones = jnp.ones((SUM_ROWS, t), BF16)

    def scores(j, s_ref, mx_ref):
        kr = kr_ref[j]
        for hh in range(hps):
            k_blk = jnp.concatenate([kn_ref[j, :, hh * LANES:(hh + 1) * LANES], kr], axis=1)
            s = _dot(k_blk, q_ref[hh * 2 * LANES:(hh + 1) * 2 * LANES, :])
            s_ref[hh] = s
            mx_ref[hh] = jnp.max(s, axis=0, keepdims=True)

    def update(j, s_ref, mx_ref, masked):
        if masked:
            keep = (lax.broadcasted_iota(jnp.int32, (t, t), 0) // CHUNK
                    <= lax.broadcasted_iota(jnp.int32, (t, t), 1) // CHUNK)
        for hh in range(hps):
            s = s_ref[hh]
            if masked:
                s = jnp.where(keep, s, NEG)
                m_blk = jnp.max(s, axis=0, keepdims=True)
            else:
                m_blk = mx_ref[hh]
            m_prev = m_ref[hh]
            m_new = jnp.maximum(m_prev, m_blk)
            alpha = jnp.exp2(m_prev - m_new)
            p = jnp.exp2(s - m_new).astype(BF16)
            v_aug = jnp.concatenate([v_ref[j, hh * LANES:(hh + 1) * LANES, :], ones], axis=0)
            acc_ref[hh] = alpha * acc_ref[hh] + _dot(v_aug, p)
            m_ref[hh] = m_new

    scores(0, sa_ref, mxa_ref)

    def pair(i, _):
        scores(2 * i + 1, sb_ref, mxb_ref)
        update(2 * i, sa_ref, mxa_ref, False)
        scores(2 * i + 2, sa_ref, mxa_ref)
        update(2 * i + 1, sb_ref, mxb_ref, False)
        return 0

    lax.fori_loop(0, qi // 2, pair, 0)

    @pl.when(qi % 2 == 0)
    def _():
        update(qi, sa_ref, mxa_ref, True)

    @pl.when(qi % 2 == 1)
    def _():
        scores(qi, sb_ref, mxb_ref)
        update(qi - 1, sa_ref, mxa_ref, False)
        update(qi, sb_ref, mxb_ref, True)

    for hh in range(hps):
        o = acc_ref[hh, :LANES, :] / acc_ref[hh, LANES:LANES + 1, :]
        o_ref[:, hh * LANES:(hh + 1) * LANES] = o.T.astype(BF16)


def _attn_prompt(qt, kn, kr, vt3, t, hps):
    n, hw = kn.shape
    nb = n // t
    kn3 = kn.reshape(nb, t, hw)
    kr3 = kr.reshape(nb, t, LANES)
    return pl.pallas_call(
        functools.partial(_attn_p_kernel, t=t, hps=hps),
        grid=(MLA_HEADS // hps, nb),
        in_specs=[
            pl.BlockSpec((hps * 2 * LANES, t), lambda h, i: (h, i)),
            pl.BlockSpec((nb, t, hps * LANES), lambda h, i: (0, 0, h), pipeline_mode=pl.Buffered(1)),
            pl.BlockSpec((nb, t, LANES), lambda h, i: (0, 0, 0), pipeline_mode=pl.Buffered(1)),
            pl.BlockSpec((nb, hps * LANES, t), lambda h, i: (0, h, 0), pipeline_mode=pl.Buffered(1)),
        ],
        out_specs=pl.BlockSpec((t, hps * LANES), lambda h, i: (i, h)),
        out_shape=jax.ShapeDtypeStruct((n, hw), BF16),
        scratch_shapes=[pltpu.VMEM((hps, 1, t), F32),
                        pltpu.VMEM((hps, LANES + SUM_ROWS, t), F32),
                        pltpu.VMEM((hps, t, t), F32), pltpu.VMEM((hps, t, t), F32),
                        pltpu.VMEM((hps, 1, t), F32), pltpu.VMEM((hps, 1, t), F32)],
        compiler_params=_params("arbitrary", "arbitrary"),
        name="attn_prompt",
    )(qt, kn3, kr3, vt3)


def _attn_s_kernel(qn_ref, qr_ref, latp_ref, krp_ref, latn_ref, krn_ref, wuk_ref, wuv_ref, o_ref,
                   *, past, ts):
    lat = jnp.concatenate([latp_ref[0].astype(BF16), latn_ref[...].astype(BF16)], axis=0)
    kr = jnp.concatenate([krp_ref[0].astype(BF16), krn_ref[...].astype(BF16)], axis=0)
    q_lat, q_rope = [], []
    for h in range(MLA_HEADS):
        sl = slice(h * LANES, (h + 1) * LANES)
        q_lat.append(_dot_nt(qn_ref[:, sl], wuk_ref[:, sl]).astype(BF16))
        q_rope.append(qr_ref[:, h * LANES:h * LANES + MLA_ROPE])
    q_lat = jnp.concatenate(q_lat, axis=0)
    q_rope = jnp.concatenate(q_rope, axis=0)
    s = _dot_nt(q_lat, lat) + _dot_nt(q_rope, kr)
    if (past + ts - 1) // CHUNK > past // CHUNK:
        n_keys = past + ts
        rows = lax.broadcasted_iota(jnp.int32, (MLA_HEADS * ts, n_keys), 0)
        keys = lax.broadcasted_iota(jnp.int32, (MLA_HEADS * ts, n_keys), 1)
        s = jnp.where(keys // CHUNK <= (past + rows % ts) // CHUNK, s, NEG)
    p = jnp.exp2(s - jnp.max(s, axis=1, keepdims=True))
    l = jnp.sum(p, axis=1, keepdims=True)
    o_lat = (_dot(p.astype(BF16), lat) / l).astype(BF16)
    for h in range(MLA_HEADS):
        sl = slice(h * LANES, (h + 1) * LANES)
        o_ref[:, sl] = _dot(o_lat[h * ts:(h + 1) * ts], wuv_ref[:, sl]).astype(BF16)


def _attn_sample(qn, qr, lat_past, kr_past, lat_new, kr_new, wuk, wuv, ts):
    n, hw = qn.shape
    nb, past, _ = lat_past.shape
    new = lambda w: pl.BlockSpec((ts, w), lambda b: (b, 0))
    old = lambda w: pl.BlockSpec((1, past, w), lambda b: (b, 0, 0))
    return pl.pallas_call(
        functools.partial(_attn_s_kernel, past=past, ts=ts),
        grid=(nb,),
        in_specs=[new(hw), new(hw), old(MLA_KV_RANK), old(MLA_ROPE), new(MLA_KV_RANK),
                  new(MLA_ROPE), _const_spec(wuk.shape), _const_spec(wuv.shape)],
        out_specs=new(hw),
        out_shape=jax.ShapeDtypeStruct((n, hw), BF16),
        compiler_params=_params("arbitrary"),
        name="attn_sample",
    )(qn, qr, lat_past, kr_past, lat_new, kr_new, wuk, wuv)


def _post_kernel(x_ref, mix_ref, xq_ref, mk_ref, mv_ref, wo_ref, g1_ref, g2_ref, wup_ref,
                 wdn_ref, g3_ref, y_ref, *, nb, tb):
    cross = []
    for b in range(nb):
        rows = slice(b * tb, (b + 1) * tb)
        heads = []
        for h in range(X_HEADS):
            sl = slice(h * X_DIM, (h + 1) * X_DIM)
            s = _dot_nt(xq_ref[rows, sl], mk_ref[b, :, sl])
            p = jnp.exp2(s - jnp.max(s, axis=1, keepdims=True))
            l = jnp.sum(p, axis=1, keepdims=True)
            heads.append((_dot(p.astype(BF16), mv_ref[b, :, sl]) / l).astype(BF16))
        cross.append(jnp.concatenate(heads, axis=1))
    cross = cross[0] if nb == 1 else jnp.concatenate(cross, axis=0)
    cat = jnp.concatenate([mix_ref[...], cross], axis=1)
    x1 = x_ref[...] + _rms(_dot(cat, wo_ref[...]), g1_ref[...])

    hb = _rms(x1, g2_ref[...]).astype(BF16)
    d_ff = wup_ref.shape[1]
    step = 1024
    acc = None
    for c in range(d_ff // step):
        up = jnp.maximum(_dot(hb, wup_ref[:, c * step:(c + 1) * step]), 0.0)
        dn = _dot((up * up).astype(BF16), wdn_ref[c * step:(c + 1) * step, :])
        acc = dn if acc is None else acc + dn
    y_ref[...] = x1 + _rms(acc, g3_ref[...])


def _post(x, mix, xq, mk, mv, wo, g1, g2, wup, wdn, g3, tm, tb):
    n, d = x.shape
    nb = tm // tb
    n_mem, xw = mk.shape[1], mk.shape[2]
    row = lambda w: pl.BlockSpec((tm, w), lambda i: (i, 0))
    if mk.shape[0] * tb == n:
        mem = pl.BlockSpec((nb, n_mem, xw), lambda i: (i, 0, 0))
    else:
        assert mk.shape[0] == 1 and nb == 1
        mem = pl.BlockSpec((1, n_mem, xw), lambda i: (0, 0, 0))
    return pl.pallas_call(
        functools.partial(_post_kernel, nb=nb, tb=tb),
        grid=(n // tm,),
        in_specs=[row(d), row(mix.shape[1]), row(xw), mem, mem, _const_spec(wo.shape),
                  _const_spec(g1.shape), _const_spec(g2.shape), _const_spec(wup.shape),
                  _const_spec(wdn.shape), _const_spec(g3.shape)],
        out_specs=row(d),
        out_shape=jax.ShapeDtypeStruct((n, d), F32),
        compiler_params=_params("arbitrary"),
        name="post",
    )(x, mix, xq, mk, mv, wo, g1, g2, wup, wdn, g3)


def _pre1_kernel(x_ref, g_ref, w_ref, lb_ref,
                 qs_ref, lf_ref, kk_ref, v_ref, gs_ref, xq_ref, *, layer):
    hb = _rms(x_ref[...], g_ref[...]).astype(BF16)
    proj = _dot(hb, w_ref[...])
    w = HG_HEADS * HG_DIM
    q, f, i, g, xq = (proj[:, :w], proj[:, w:2 * w], proj[:, 2 * w:3 * w],
                      proj[:, 3 * w:4 * w], proj[:, 4 * w:])
    lb_all = lb_ref[...]
    e = jnp.exp(lb_all - jnp.max(lb_all, axis=0, keepdims=True))
    soft = e / jnp.sum(e, axis=0, keepdims=True)
    lb = jnp.sum(soft[:layer + 1], axis=0, keepdims=True) - soft[0:1]
    forget = lb + (1.0 - lb) * _sigmoid(f)
    qs_ref[...] = q * _sigmoid(q)
    lf_ref[...] = jnp.log(forget)
    kk_ref[...] = (1.0 - lb) * _sigmoid(-f)
    v_ref[...] = i.astype(BF16)
    gs_ref[...] = g * _sigmoid(g)
    xq_ref[...] = (xq * (X_DIM ** -0.5 * LOG2E)).astype(BF16)


def _pre1(x, g, w, lb, layer, tm):
    n, d = x.shape
    hw = HG_HEADS * HG_DIM
    xw = X_HEADS * X_DIM
    row = lambda wd: pl.BlockSpec((tm, wd), lambda i: (i, 0))
    f32o = jax.ShapeDtypeStruct((n, hw), F32)
    return pl.pallas_call(
        functools.partial(_pre1_kernel, layer=layer),
        grid=(n // tm,),
        in_specs=[row(d), _const_spec(g.shape), _const_spec(w.shape), _const_spec(lb.shape)],
        out_specs=[row(hw), row(hw), row(hw), row(hw), row(hw), row(xw)],
        out_shape=[f32o, f32o, f32o, jax.ShapeDtypeStruct((n, hw), BF16), f32o,
                   jax.ShapeDtypeStruct((n, xw), BF16)],
        compiler_params=_params("arbitrary"),
        name="pre1",
    )(x, g, w, lb)


def _split3(x):
    hi = x.astype(BF16)
    r1 = x - hi.astype(F32)
    mid = r1.astype(BF16)
    lo = (r1 - mid.astype(F32)).astype(BF16)
    return hi, mid, lo


def _gla_kernel(qs_ref, lf_ref, kk_ref, v_ref, gs_ref, s0_ref, gn_ref, mix_ref, sfin_ref,
                st_ref, *, cb):
    blk = pl.program_id(1)
    t = cb * CHUNK
    d = HG_DIM

    @pl.when(blk == 0)
    def _():
        for h in range(HG_HEADS):
            st_ref[h] = s0_ref[0, h].T

    row = lax.broadcasted_iota(jnp.int32, (t, t), 0)
    col = lax.broadcasted_iota(jnp.int32, (t, t), 1)
    causal = jnp.logical_and(row // CHUNK == col // CHUNK, col <= row)
    tri = jnp.where(causal, 1.0, 0.0).astype(BF16)
    gn = gn_ref[...]
    hw = HG_HEADS * d
    heads = [slice(h * d, (h + 1) * d) for h in range(HG_HEADS)]

    bb = _dot(tri, jnp.concatenate(_split3(lf_ref[...]), axis=1))
    b = (bb[:, :hw] + bb[:, hw:2 * hw]) + bb[:, 2 * hw:]
    bcast = lambda r: jnp.concatenate(
        [jnp.broadcast_to(b[c * CHUNK + r:c * CHUNK + r + 1, :], (CHUNK, hw)) for c in range(cb)],
        axis=0)
    ref = bcast(CHUNK // 2)
    last = bcast(CHUNK - 1)
    qs = qs_ref[...]
    kk = kk_ref[...]
    v = v_ref[...]
    qe = (qs * jnp.exp(b - ref)).astype(BF16)
    ke = (kk * jnp.exp(ref - b)).astype(BF16)
    kh = (kk * jnp.exp(last - b)).astype(BF16)
    qb = (qs * jnp.exp(b)).astype(BF16)
    a = [jnp.where(causal, _dot_nt(qe[:, sl], ke[:, sl]), 0.0).astype(BF16) for sl in heads]
    o = [_dot(a[h], v[:, sl]) for h, sl in enumerate(heads)]
    st = [st_ref[h] for h in range(HG_HEADS)]
    inter = [[] for _ in heads]
    for c in range(cb):
        rows = slice(c * CHUNK, (c + 1) * CHUNK)
        decay = jnp.exp(b[c * CHUNK + CHUNK - 1:c * CHUNK + CHUNK, :])
        for h, sl in enumerate(heads):
            inter[h].append(_dot_nt(qb[rows, sl], st[h].astype(BF16)))
        for h, sl in enumerate(heads):
            st[h] = st[h] * decay[:, sl] + _dot_tn(v[rows, sl], kh[rows, sl])
    for h, sl in enumerate(heads):
        st_ref[h] = st[h]
        o_h = o[h] + (inter[h][0] if cb == 1 else jnp.concatenate(inter[h], axis=0))
        mix_ref[:, sl] = (_rms(o_h, gn) * gs_ref[:, sl]).astype(BF16)

    @pl.when(blk == pl.num_programs(1) - 1)
    def _():
        for h in range(HG_HEADS):
            sfin_ref[0, h] = st_ref[h].T


def _gla(qs, lf, kk, v, gs, s0, gn, nbatch, cb):
    n, hw = qs.shape
    t = cb * CHUNK
    nblk = n // nbatch // t
    row = pl.BlockSpec((t, hw), lambda b, i: (b * nblk + i, 0))
    st = pl.BlockSpec((1, HG_HEADS, HG_DIM, HG_DIM), lambda b, i: (b, 0, 0, 0))
    return pl.pallas_call(
        functools.partial(_gla_kernel, cb=cb),
        grid=(nbatch, nblk),
        in_specs=[row, row, row, row, row, st, _const_spec(gn.shape)],
        out_specs=[row, st],
        out_shape=[jax.ShapeDtypeStruct((n, hw), BF16),
                   jax.ShapeDtypeStruct(s0.shape, F32)],
        scratch_shapes=[pltpu.VMEM((HG_HEADS, HG_DIM, HG_DIM), F32)],
        compiler_params=_params("arbitrary", "arbitrary"),
        name="gla",
    )(qs, lf, kk, v, gs, s0, gn)


def _rope_tables(pos):
    half = MLA_ROPE // 2
    inv = jnp.power(ROPE_THETA, -jnp.arange(half, dtype=F32) / half)
    ang = pos.astype(F32)[:, None] * inv[None, :]
    cos, sin = jnp.cos(ang), jnp.sin(ang)
    z = jnp.zeros_like(cos)
    c = jnp.concatenate([cos, cos, z, z], axis=1)
    s1 = jnp.concatenate([z, sin, z, z], axis=1)
    s2 = jnp.concatenate([-sin, z, z, z], axis=1)
    return (c, s1, s2), (cos.T, sin.T)


def _prep_weights(mla_w_in, mla_w_uq, mla_w_uk, mla_w_uv):
    o1 = MLA_Q_RANK + MLA_KV_RANK
    o2 = o1 + MLA_ROPE
    d = mla_w_in.shape[0]
    win = jnp.concatenate([mla_w_in[:, :o1], mla_w_in[:, o2:], mla_w_in[:, o1:o2],
                           jnp.zeros((d, LANES - MLA_ROPE), mla_w_in.dtype)], axis=1)
    wq = mla_w_uq.reshape(MLA_Q_RANK, MLA_HEADS, MLA_NOPE + MLA_ROPE)
    nope = wq[:, :, :MLA_NOPE].reshape(MLA_Q_RANK, MLA_HEADS * MLA_NOPE)
    rope = jnp.pad(wq[:, :, MLA_NOPE:], ((0, 0), (0, 0), (0, LANES - MLA_ROPE)))
    wuq = jnp.concatenate([nope, rope.reshape(MLA_Q_RANK, MLA_HEADS * LANES)], axis=1)
    wuk = mla_w_uk.reshape(MLA_KV_RANK, MLA_HEADS * MLA_NOPE)
    wuv = mla_w_uv.reshape(MLA_KV_RANK, MLA_HEADS * MLA_V)
    wq_t = jnp.pad(wq, ((0, 0), (0, 0), (0, LANES - MLA_ROPE))).reshape(MLA_Q_RANK, -1).T
    bf = lambda a: a.astype(BF16)
    return bf(win), bf(wuq), bf(wuk), bf(wuv), bf(wq_t), bf(wuv.T)


def kernel(x_prompt, x_sample, cache_mla_latent, cache_mla_krope, cache_hgrn_state, cache_mem_k,
           cache_mem_v, mem_prompt, ln_mix_pre, ln_mix_post, ln_ffn_pre, ln_ffn_post, mem_norm,
           w_mem_kv, mla_w_in, mla_q_norm, mla_kv_norm, mla_w_uq, mla_w_uk, mla_w_uv, mla_w_out,
           hgrn_w_in, hgrn_lb, hgrn_o_norm, hgrn_w_out, w_ffn_up, w_ffn_down):
    bp, tp, d = x_prompt.shape
    bs, ts, _ = x_sample.shape
    depth = ln_mix_pre.shape[0]
    past = cache_mla_latent.shape[2]
    n_mem = mem_prompt.shape[1]
    xw = X_HEADS * X_DIM
    assert depth == 2 and bp == 1 and ts == CHUNK and past % CHUNK == 0
    assert MLA_NOPE == LANES and MLA_V == LANES and HG_DIM == LANES and X_DIM == LANES

    row2 = lambda a: a.reshape(1, -1)
    win0, wuq, wuk, wuv, wuq_t, wuv_t = _prep_weights(mla_w_in[0], mla_w_uq[0], mla_w_uk[0],
                                                      mla_w_uv[0])
    wout = (mla_w_out[0].astype(BF16), hgrn_w_out[0].astype(BF16))
    wup = w_ffn_up.astype(BF16)
    wdn = w_ffn_down.astype(BF16)
    win1 = hgrn_w_in[0].astype(BF16)

    mem_k_p, mem_v_p = _memkv(mem_prompt, mem_norm, w_mem_kv.astype(BF16))

    def trunk(x, pos, mem_k, mem_v, s0, nbatch, tm, tb, history):
        n = x.shape[0]
        tok_tables, lane_tables = _rope_tables(pos)
        pre0_args = (x, row2(ln_mix_pre[0]), win0, row2(mla_q_norm[0]), row2(mla_kv_norm[0]))
        if history is None:
            qt, kn, kr, vt3, lat, krope, xq = _pre0(
                *pre0_args, wuq_t, tok_tables + lane_tables, ATTN_TILE, (wuk, wuv_t))
            mix = _attn_prompt(qt, kn, kr, vt3, ATTN_TILE, ATTN_HEADS_PER_STEP)
        else:
            qn, qr, lat, krope, xq = _pre0(*pre0_args, wuq, tok_tables, tm)
            mix = _attn_sample(qn, qr, history[0], history[1], lat, krope, wuk, wuv, ts)
        x = _post(x, mix, xq, mem_k[0].astype(BF16), mem_v[0].astype(BF16), wout[0],
                  row2(ln_mix_post[0]), row2(ln_ffn_pre[0]), wup[0], wdn[0],
                  row2(ln_ffn_post[0]), tm, tb)
        qs, lf, kk, vv, gs, xq = _pre1(x, row2(ln_mix_pre[1]), win1, hgrn_lb, 1, 256)
        mix, s_fin = _gla(qs, lf, kk, vv, gs, s0, row2(hgrn_o_norm[0]), nbatch,
                          4 if n // nbatch >= 4 * CHUNK else 1)
        x = _post(x, mix, xq, mem_k[1].astype(BF16), mem_v[1].astype(BF16), wout[1],
                  row2(ln_mix_post[1]), row2(ln_ffn_pre[1]), wup[1], wdn[1],
                  row2(ln_ffn_post[1]), tm, tb)
        return x, lat, krope, s_fin

    mk_p = mem_k_p.reshape(depth, bp, n_mem, xw)
    mv_p = mem_v_p.reshape(depth, bp, n_mem, xw)
    s0_p = jnp.zeros((bp, HG_HEADS, HG_DIM, HG_DIM), F32)
    y_p, lat_p, kr_p, st_p = trunk(x_prompt.reshape(bp * tp, d), jnp.arange(tp), mk_p, mv_p,
                                   s0_p, bp, 512, 512, None)
    pos_s = jnp.tile(past + jnp.arange(ts), bs)
    mk_s = cache_mem_k.reshape(depth, bs, n_mem, xw)
    mv_s = cache_mem_v.reshape(depth, bs, n_mem, xw)
    y_s, lat_s, kr_s, st_s = trunk(x_sample.reshape(bs * ts, d), pos_s, mk_s, mv_s,
                                   cache_hgrn_state[0], bs, 4 * ts, ts,
                                   (cache_mla_latent[0], cache_mla_krope[0]))

    return (y_p.reshape(bp, tp, d), y_s.reshape(bs, ts, d),
            lat_p.reshape(1, bp, tp, MLA_KV_RANK), kr_p.reshape(1, bp, tp, MLA_ROPE),
            st_p.reshape(1, bp, HG_HEADS, HG_DIM, HG_DIM),
            mem_k_p.reshape(depth, bp, n_mem, X_HEADS, X_DIM),
            mem_v_p.reshape(depth, bp, n_mem, X_HEADS, X_DIM),
            lat_s.reshape(1, bs, ts, MLA_KV_RANK), kr_s.reshape(1, bs, ts, MLA_ROPE),
            st_s.reshape(1, bs, HG_HEADS, HG_DIM, HG_DIM))
```

```python
import functools

import jax
import jax.numpy as jnp
from jax import lax
from jax.experimental import pallas as pl
from jax.experimental.pallas import tpu as pltpu

F32 = jnp.float32
BF16 = jnp.bfloat16

CHUNK = 64
EPS = 1e-6
ROPE_THETA = 10000.0
NEG = -1e30
MLA_HEADS = 8
MLA_NOPE = 128
MLA_ROPE = 64
MLA_V = 128
MLA_Q_RANK = 384
MLA_KV_RANK = 256
HG_HEADS = 8
HG_DIM = 128
X_HEADS = 4
X_DIM = 128
LANES = 128
LOG2E = 1.4426950408889634

VMEM_LIMIT_BYTES = 52 * 1024 * 1024

TOKEN_TILE = 512
PRE1_TILE = 256
ATTN_TILE = 512
ATTN_HEADS_PER_STEP = 2
SUM_ROWS = 16


def _dot(a, b):
    return jnp.dot(a, b, preferred_element_type=F32)


def _dot_nt(a, b):
    return lax.dot_general(a, b, (((1,), (1,)), ((), ())), preferred_element_type=F32)


def _dot_tn(a, b):
    return lax.dot_general(a, b, (((0,), (0,)), ((), ())), preferred_element_type=F32)


def _rms(x, g):
    ms = jnp.mean(x * x, axis=-1, keepdims=True)
    return x * lax.rsqrt(ms + EPS) * g


def _sigmoid(x):
    return 1.0 / (1.0 + jnp.exp(-x))


def _params(*sem):
    return pltpu.CompilerParams(dimension_semantics=sem, vmem_limit_bytes=VMEM_LIMIT_BYTES)


def _const_spec(shape):
    nd = len(shape)
    return pl.BlockSpec(shape, lambda *_: (0,) * nd, pipeline_mode=pl.Buffered(1))


def _layer_spec(shape, layer):
    nd = len(shape) - 1
    return pl.BlockSpec((None,) + tuple(shape[1:]), lambda *_: (layer,) + (0,) * nd,
                        pipeline_mode=pl.Buffered(1))


def _memkv_kernel(mem_ref, g_ref, w_ref, k_ref, v_ref):
    h = _rms(mem_ref[0], g_ref[0]).astype(BF16)
    kv = _dot(h, w_ref[0])
    half = kv.shape[1] // 2
    k_ref[0, 0] = kv[:, :half]
    v_ref[0, 0] = kv[:, half:]


def _memkv(mem, g, w):
    bp, n_mem, d = mem.shape
    depth = g.shape[0]
    width = w.shape[2] // 2
    out = jax.ShapeDtypeStruct((depth, bp, n_mem, width), F32)
    return pl.pallas_call(
        _memkv_kernel,
        grid=(depth, bp),
        in_specs=[
            pl.BlockSpec((1, n_mem, d), lambda l, b: (b, 0, 0)),
            pl.BlockSpec((1, 1, d), lambda l, b: (l, 0, 0)),
            pl.BlockSpec((1, d, 2 * width), lambda l, b: (l, 0, 0)),
        ],
        out_specs=[
            pl.BlockSpec((1, 1, n_mem, width), lambda l, b: (l, b, 0, 0)),
            pl.BlockSpec((1, 1, n_mem, width), lambda l, b: (l, b, 0, 0)),
        ],
        out_shape=[out, out],
        compiler_params=_params("arbitrary", "arbitrary"),
        name="memkv",
    )(mem, g.reshape(depth, 1, d), w)


def _rope_rot(x, c, s1, s2):
    w = x.shape[1]
    half = MLA_ROPE // 2
    return x * c + pltpu.roll(x, half, 1) * s1 + pltpu.roll(x, w - half, 1) * s2


def _pre0_kernel(*refs, expand):
    (x_ref, g_ref, win_ref, qg_ref, kvg_ref, wq_ref,
     cos_ref, sin_ref, cos0_ref, sin0_ref) = refs[:10]
    if expand:
        (wuk_ref, wv_ref, cos_t_ref, sin_t_ref, cos0_t_ref, sin0_t_ref,
         q_ref, kn_ref, kr_ref, v_ref, lat_ref, krope_ref, xq_ref) = refs[10:]
    else:
        qn_ref, qr_ref, lat_ref, krope_ref, xq_ref = refs[10:]
    h = _rms(x_ref[...], g_ref[...]).astype(BF16)
    proj = _dot(h, win_ref[...])
    o1 = MLA_Q_RANK
    o2 = o1 + MLA_KV_RANK
    o3 = o2 + X_HEADS * X_DIM
    c_q, c_kv, xq, k_r = proj[:, :o1], proj[:, o1:o2], proj[:, o2:o3], proj[:, o3:]

    scale = (MLA_NOPE + MLA_ROPE) ** -0.5 * LOG2E
    qc = _rms(c_q, qg_ref[...]).astype(BF16)
    half = MLA_ROPE // 2
    cos0, sin0 = cos0_ref[0], sin0_ref[0]
    c = cos0 * cos_ref[...] - sin0 * sin_ref[...]
    sn = sin0 * cos_ref[...] + cos0 * sin_ref[...]
    lane = lax.broadcasted_iota(jnp.int32, sn.shape, 1)
    s1 = jnp.where(lane >= half, sn, 0.0)
    s2 = jnp.where(lane < half, -sn, 0.0)
    lat = _rms(c_kv, kvg_ref[...])
    lat_ref[...] = lat
    kr = _rope_rot(k_r, c, s1, s2)
    krope_ref[...] = kr[:, :MLA_ROPE]
    xq_ref[...] = (xq * (X_DIM ** -0.5 * LOG2E)).astype(BF16)
    if expand:
        latb = lat.astype(BF16)
        kn_ref[...] = _dot(latb, wuk_ref[...]).astype(BF16)
        kr_ref[...] = kr.astype(BF16)
        qt = _dot_nt(wq_ref[...], qc) * scale
        lanes4 = lambda a: jnp.concatenate([a] * (qt.shape[1] // LANES), axis=1)
        cos0_t, sin0_t = lanes4(cos0_t_ref[0]), lanes4(sin0_t_ref[0])
        cos = cos0_t * cos_t_ref[...] - sin0_t * sin_t_ref[...]
        sin = sin0_t * cos_t_ref[...] + cos0_t * sin_t_ref[...]
        slabs = []
        for hd in range(MLA_HEADS):
            r0 = hd * 2 * LANES + MLA_NOPE
            x1, x2 = qt[r0:r0 + half], qt[r0 + half:r0 + 2 * half]
            slabs += [qt[hd * 2 * LANES:r0], x1 * cos - x2 * sin, x2 * cos + x1 * sin,
                      qt[r0 + 2 * half:(hd + 1) * 2 * LANES]]
        q_ref[...] = jnp.concatenate(slabs, axis=0).astype(BF16)
        v_ref[0] = _dot_nt(wv_ref[...], latb).astype(BF16)
    else:
        q = _dot(qc, wq_ref[...]) * scale
        nope_w = MLA_HEADS * MLA_NOPE
        qn_ref[...] = q[:, :nope_w].astype(BF16)
        tile = lambda t: jnp.concatenate([t] * MLA_HEADS, axis=1)
        qr_ref[...] = _rope_rot(q[:, nope_w:], tile(c), tile(s1), tile(s2)).astype(BF16)


def _pre0(x, g, win, qg, kvg, wq, tables, tm, expand_weights=None):
    n, d = x.shape
    hw = MLA_HEADS * LANES
    xw = X_HEADS * X_DIM
    expand = expand_weights is not None
    row = lambda w: pl.BlockSpec((tm, w), lambda i: (i, 0))
    col = lambda r: pl.BlockSpec((r, tm), lambda i: (0, i))
    sds = jax.ShapeDtypeStruct
    in_specs = [row(d), _const_spec(g.shape), _const_spec(win.shape), _const_spec(qg.shape),
                _const_spec(kvg.shape), _const_spec(wq.shape),
                _const_spec((tm, LANES)), _const_spec((tm, LANES)),
                pl.BlockSpec((1, 1, LANES), lambda i: (i, 0, 0)),
                pl.BlockSpec((1, 1, LANES), lambda i: (i, 0, 0))]
    common_specs = [row(MLA_KV_RANK), row(MLA_ROPE), row(xw)]
    common_shapes = [sds((n, MLA_KV_RANK), F32),
                     sds((n, MLA_ROPE), F32),
                     sds((n, xw), BF16)]
    if expand:
        half = MLA_ROPE // 2
        in_specs += [_const_spec(w.shape) for w in expand_weights]
        in_specs += [_const_spec((half, tm)), _const_spec((half, tm)),
                     pl.BlockSpec((1, half, LANES), lambda i: (i, 0, 0)),
                     pl.BlockSpec((1, half, LANES), lambda i: (i, 0, 0))]
        out_specs = [col(2 * hw), row(hw), row(LANES), pl.BlockSpec((1, hw, tm), lambda i: (i, 0, 0))]
        out_shape = [sds((2 * hw, n), BF16),
                     sds((n, hw), BF16),
                     sds((n, LANES), BF16),
                     sds((n // tm, hw, tm), BF16)]
        operands = (x, g, win, qg, kvg, wq) + tuple(tables[:4]) + tuple(expand_weights) + tuple(tables[4:])
    else:
        out_specs = [row(hw), row(hw)]
        out_shape = [sds((n, hw), BF16),
                     sds((n, hw), BF16)]
        operands = (x, g, win, qg, kvg, wq) + tuple(tables[:4])
    return pl.pallas_call(
        functools.partial(_pre0_kernel, expand=expand),
        grid=(n // tm,),
        in_specs=in_specs,
        out_specs=out_specs + common_specs,
        out_shape=out_shape + common_shapes,
        compiler_params=_params("arbitrary"),
        name="pre0",
    )(*operands)


def _attn_p_kernel(q_ref, kn_ref, kr_ref, v_ref, o_ref, m_ref, acc_ref, sa_ref, sb_ref,
                   mxa_ref, mxb_ref, *, t, hps):
    qi = pl.program_id(1)
    m_ref[...] = jnp.full(m_ref.shape, NEG, F32)
    acc_ref[...] = jnp.zeros(acc_ref.shape, F32)
    ones = jnp.ones((SUM_ROWS, t), BF16)

    def scores(j, s_ref, mx_ref):
        kr = kr_ref[j]
        for hh in range(hps):
            k_blk = jnp.concatenate([kn_ref[j, :, hh * LANES:(hh + 1) * LANES], kr], axis=1)
            s = _dot(k_blk, q_ref[hh * 2 * LANES:(hh + 1) * 2 * LANES, :])
            s_ref[hh] = s
            mx_ref[hh] = jnp.max(s, axis=0, keepdims=True)

    def update(j, s_ref, mx_ref, masked):
        if masked:
            keep = (lax.broadcasted_iota(jnp.int32, (t, t), 0) // CHUNK
                    <= lax.broadcasted_iota(jnp.int32, (t, t), 1) // CHUNK)
        for hh in range(hps):
            s = s_ref[hh]
            if masked:
                s = jnp.where(keep, s, NEG)
                m_blk = jnp.max(s, axis=0, keepdims=True)
            else:
                m_blk = mx_ref[hh]
            m_prev = m_ref[hh]
            m_new = jnp.maximum(m_prev, m_blk)
            alpha = jnp.exp2(m_prev - m_new)
            p = jnp.exp2(s - m_new).astype(BF16)
            v_aug = jnp.concatenate([v_ref[j, hh * LANES:(hh + 1) * LANES, :], ones], axis=0)
            acc_ref[hh] = alpha * acc_ref[hh] + _dot(v_aug, p)
            m_ref[hh] = m_new

    scores(0, sa_ref, mxa_ref)

    def pair(i, _):
        scores(2 * i + 1, sb_ref, mxb_ref)
        update(2 * i, sa_ref, mxa_ref, False)
        scores(2 * i + 2, sa_ref, mxa_ref)
        update(2 * i + 1, sb_ref, mxb_ref, False)
        return 0

    lax.fori_loop(0, qi // 2, pair, 0)

    @pl.when(qi % 2 == 0)
    def _():
        update(qi, sa_ref, mxa_ref, True)

    @pl.when(qi % 2 == 1)
    def _():
        scores(qi, sb_ref, mxb_ref)
        update(qi - 1, sa_ref, mxa_ref, False)
        update(qi, sb_ref, mxb_ref, True)

    for hh in range(hps):
        o = acc_ref[hh, :LANES, :] / acc_ref[hh, LANES:LANES + 1, :]
        o_ref[:, hh * LANES:(hh + 1) * LANES] = o.T.astype(BF16)


def _attn_prompt(qt, kn, kr, vt3, t, hps):
    n, hw = kn.shape
    nb = n // t
    kn3 = kn.reshape(nb, t, hw)
    kr3 = kr.reshape(nb, t, LANES)
    return pl.pallas_call(
        functools.partial(_attn_p_kernel, t=t, hps=hps),
        grid=(MLA_HEADS // hps, nb),
        in_specs=[
            pl.BlockSpec((hps * 2 * LANES, t), lambda h, i: (h, i)),
            pl.BlockSpec((nb, t, hps * LANES), lambda h, i: (0, 0, h), pipeline_mode=pl.Buffered(1)),
            pl.BlockSpec((nb, t, LANES), lambda h, i: (0, 0, 0), pipeline_mode=pl.Buffered(1)),
            pl.BlockSpec((nb, hps * LANES, t), lambda h, i: (0, h, 0), pipeline_mode=pl.Buffered(1)),
        ],
        out_specs=pl.BlockSpec((t, hps * LANES), lambda h, i: (i, h)),
        out_shape=jax.ShapeDtypeStruct((n, hw), BF16),
        scratch_shapes=[pltpu.VMEM((hps, 1, t), F32),
                        pltpu.VMEM((hps, LANES + SUM_ROWS, t), F32),
                        pltpu.VMEM((hps, t, t), F32), pltpu.VMEM((hps, t, t), F32),
                        pltpu.VMEM((hps, 1, t), F32), pltpu.VMEM((hps, 1, t), F32)],
        compiler_params=_params("arbitrary", "arbitrary"),
        name="attn_prompt",
    )(qt, kn3, kr3, vt3)


def _attn_s_kernel(qn_ref, qr_ref, latp_ref, krp_ref, latn_ref, krn_ref, wuk_ref, wuv_ref, o_ref,
                   *, past, ts):
    lat = jnp.concatenate([latp_ref[0].astype(BF16), latn_ref[...].astype(BF16)], axis=0)
    kr = jnp.concatenate([krp_ref[0].astype(BF16), krn_ref[...].astype(BF16)], axis=0)
    q_lat, q_rope = [], []
    for h in range(MLA_HEADS):
        sl = slice(h * LANES, (h + 1) * LANES)
        q_lat.append(_dot_nt(qn_ref[:, sl], wuk_ref[:, sl]).astype(BF16))
        q_rope.append(qr_ref[:, h * LANES:h * LANES + MLA_ROPE])
    q_lat = jnp.concatenate(q_lat, axis=0)
    q_rope = jnp.concatenate(q_rope, axis=0)
    s = _dot_nt(q_lat, lat) + _dot_nt(q_rope, kr)
    if (past + ts - 1) // CHUNK > past // CHUNK:
        n_keys = past + ts
        rows = lax.broadcasted_iota(jnp.int32, (MLA_HEADS * ts, n_keys), 0)
        keys = lax.broadcasted_iota(jnp.int32, (MLA_HEADS * ts, n_keys), 1)
        s = jnp.where(keys // CHUNK <= (past + rows % ts) // CHUNK, s, NEG)
    p = jnp.exp2(s - jnp.max(s, axis=1, keepdims=True))
    l = jnp.sum(p, axis=1, keepdims=True)
    o_lat = (_dot(p.astype(BF16), lat) / l).astype(BF16)
    for h in range(MLA_HEADS):
        sl = slice(h * LANES, (h + 1) * LANES)
        o_ref[:, sl] = _dot(o_lat[h * ts:(h + 1) * ts], wuv_ref[:, sl]).astype(BF16)


def _attn_sample(qn, qr, lat_past, kr_past, lat_new, kr_new, wuk, wuv, ts):
    n, hw = qn.shape
    nb, past, _ = lat_past.shape
    new = lambda w: pl.BlockSpec((ts, w), lambda b: (b, 0))
    old = lambda w: pl.BlockSpec((1, past, w), lambda b: (b, 0, 0))
    return pl.pallas_call(
        functools.partial(_attn_s_kernel, past=past, ts=ts),
        grid=(nb,),
        in_specs=[new(hw), new(hw), old(MLA_KV_RANK), old(MLA_ROPE), new(MLA_KV_RANK),
                  new(MLA_ROPE), _const_spec(wuk.shape), _const_spec(wuv.shape)],
        out_specs=new(hw),
        out_shape=jax.ShapeDtypeStruct((n, hw), BF16),
        compiler_params=_params("arbitrary"),
        name="attn_sample",
    )(qn, qr, lat_past, kr_past, lat_new, kr_new, wuk, wuv)


def _post_kernel(x_ref, mix_ref, xq_ref, mk_ref, mv_ref, wo_ref, g1_ref, g2_ref, wup_ref,
                 wdn_ref, g3_ref, y_ref, *, nb, tb, interleaved):
    def mem_head(ref, b, h):
        if interleaved:
            return ref[b, pl.ds(h, ref.shape[1] // X_HEADS, stride=X_HEADS), :].astype(BF16)
        return ref[b, :, h * X_DIM:(h + 1) * X_DIM].astype(BF16)

    cross = []
    for b in range(nb):
        rows = slice(b * tb, (b + 1) * tb)
        heads = []
        for h in range(X_HEADS):
            sl = slice(h * X_DIM, (h + 1) * X_DIM)
            s = _dot_nt(xq_ref[rows, sl], mem_head(mk_ref, b, h))
            p = jnp.exp2(s - jnp.max(s, axis=1, keepdims=True))
            l = jnp.sum(p, axis=1, keepdims=True)
            heads.append((_dot(p.astype(BF16), mem_head(mv_ref, b, h)) / l).astype(BF16))
        cross.append(jnp.concatenate(heads, axis=1))
    cross = cross[0] if nb == 1 else jnp.concatenate(cross, axis=0)
    cat = jnp.concatenate([mix_ref[...], cross], axis=1)
    x1 = x_ref[...] + _rms(_dot(cat, wo_ref[...]), g1_ref[...])

    hb = _rms(x1, g2_ref[...]).astype(BF16)
    d_ff = wup_ref.shape[1]
    step = 1024
    acc = None
    for c in range(d_ff // step):
        up = jnp.maximum(_dot(hb, wup_ref[:, c * step:(c + 1) * step]), 0.0)
        dn = _dot((up * up).astype(BF16), wdn_ref[c * step:(c + 1) * step, :])
        acc = dn if acc is None else acc + dn
    y_ref[...] = x1 + _rms(acc, g3_ref[...])


def _post(x, mix, xq, mk, mv, layer, wo, g1, g2, wup, wdn, g3, tm, tb):
    n, d = x.shape
    nb = tm // tb
    xw = X_HEADS * X_DIM
    _, nmem_b, mrows, mwidth = mk.shape
    interleaved = mwidth == X_DIM
    row = lambda w: pl.BlockSpec((tm, w), lambda i: (i, 0))
    if nmem_b * tb == n:
        mem = pl.BlockSpec((None, nb, mrows, mwidth), lambda i: (layer, i, 0, 0))
    else:
        assert nmem_b == 1 and nb == 1
        mem = pl.BlockSpec((None, 1, mrows, mwidth), lambda i: (layer, 0, 0, 0))
    return pl.pallas_call(
        functools.partial(_post_kernel, nb=nb, tb=tb, interleaved=interleaved),
        grid=(n // tm,),
        in_specs=[row(d), row(mix.shape[1]), row(xw), mem, mem, _const_spec(wo.shape),
                  _const_spec(g1.shape), _const_spec(g2.shape), _layer_spec(wup.shape, layer),
                  _layer_spec(wdn.shape, layer), _const_spec(g3.shape)],
        out_specs=row(d),
        out_shape=jax.ShapeDtypeStruct((n, d), F32),
        compiler_params=_params("arbitrary"),
        name="post",
    )(x, mix, xq, mk, mv, wo, g1, g2, wup, wdn, g3)


def _pre1_kernel(x_ref, g_ref, w_ref, lb_ref,
                 qs_ref, lf_ref, kk_ref, v_ref, gs_ref, xq_ref, *, layer):
    hb = _rms(x_ref[...], g_ref[...]).astype(BF16)
    proj = _dot(hb, w_ref[...])
    w = HG_HEADS * HG_DIM
    q, f, i, g, xq = (proj[:, :w], proj[:, w:2 * w], proj[:, 2 * w:3 * w],
                      proj[:, 3 * w:4 * w], proj[:, 4 * w:])
    lb_all = lb_ref[...]
    e = jnp.exp(lb_all - jnp.max(lb_all, axis=0, keepdims=True))
    soft = e / jnp.sum(e, axis=0, keepdims=True)
    lb = jnp.sum(soft[:layer + 1], axis=0, keepdims=True) - soft[0:1]
    forget = lb + (1.0 - lb) * _sigmoid(f)
    qs_ref[...] = q * _sigmoid(q)
    lf_ref[...] = jnp.log(forget)
    kk_ref[...] = (1.0 - lb) * _sigmoid(-f)
    v_ref[...] = i.astype(BF16)
    gs_ref[...] = g * _sigmoid(g)
    xq_ref[...] = (xq * (X_DIM ** -0.5 * LOG2E)).astype(BF16)


def _pre1(x, g, w, lb, layer, tm):
    n, d = x.shape
    hw = HG_HEADS * HG_DIM
    xw = X_HEADS * X_DIM
    row = lambda wd: pl.BlockSpec((tm, wd), lambda i: (i, 0))
    f32o = jax.ShapeDtypeStruct((n, hw), F32)
    return pl.pallas_call(
        functools.partial(_pre1_kernel, layer=layer),
        grid=(n // tm,),
        in_specs=[row(d), _const_spec(g.shape), _const_spec(w.shape), _const_spec(lb.shape)],
        out_specs=[row(hw), row(hw), row(hw), row(hw), row(hw), row(xw)],
        out_shape=[f32o, f32o, f32o, jax.ShapeDtypeStruct((n, hw), BF16), f32o,
                   jax.ShapeDtypeStruct((n, xw), BF16)],
        compiler_params=_params("arbitrary"),
        name="pre1",
    )(x, g, w, lb)


def _split3(x):
    hi = x.astype(BF16)
    r1 = x - hi.astype(F32)
    mid = r1.astype(BF16)
    lo = (r1 - mid.astype(F32)).astype(BF16)
    return hi, mid, lo


def _gla_kernel(qs_ref, lf_ref, kk_ref, v_ref, gs_ref, s0_ref, gn_ref, mix_ref, sfin_ref,
                st_ref, *, cb):
    blk = pl.program_id(1)
    t = cb * CHUNK
    d = HG_DIM

    @pl.when(blk == 0)
    def _():
        for h in range(HG_HEADS):
            st_ref[h] = s0_ref[0, h].T

    row = lax.broadcasted_iota(jnp.int32, (t, t), 0)
    col = lax.broadcasted_iota(jnp.int32, (t, t), 1)
    causal = jnp.logical_and(row // CHUNK == col // CHUNK, col <= row)
    tri = jnp.where(causal, 1.0, 0.0).astype(BF16)
    gn = gn_ref[...]
    hw = HG_HEADS * d
    heads = [slice(h * d, (h + 1) * d) for h in range(HG_HEADS)]

    bb = _dot(tri, jnp.concatenate(_split3(lf_ref[...]), axis=1))
    b = (bb[:, :hw] + bb[:, hw:2 * hw]) + bb[:, 2 * hw:]
    bcast = lambda r: jnp.concatenate(
        [jnp.broadcast_to(b[c * CHUNK + r:c * CHUNK + r + 1, :], (CHUNK, hw)) for c in range(cb)],
        axis=0)
    ref = bcast(CHUNK // 2)
    last = bcast(CHUNK - 1)
    qs = qs_ref[...]
    kk = kk_ref[...]
    v = v_ref[...]
    qe = (qs * jnp.exp(b - ref)).astype(BF16)
    ke = (kk * jnp.exp(ref - b)).astype(BF16)
    kh = (kk * jnp.exp(last - b)).astype(BF16)
    qb = (qs * jnp.exp(b)).astype(BF16)
    a = [jnp.where(causal, _dot_nt(qe[:, sl], ke[:, sl]), 0.0).astype(BF16) for sl in heads]
    o = [_dot(a[h], v[:, sl]) for h, sl in enumerate(heads)]
    st = [st_ref[h] for h in range(HG_HEADS)]
    inter = [[] for _ in heads]
    for c in range(cb):
        rows = slice(c * CHUNK, (c + 1) * CHUNK)
        decay = jnp.exp(b[c * CHUNK + CHUNK - 1:c * CHUNK + CHUNK, :])
        for h, sl in enumerate(heads):
            inter[h].append(_dot_nt(qb[rows, sl], st[h].astype(BF16)))
        for h, sl in enumerate(heads):
            st[h] = st[h] * decay[:, sl] + _dot_tn(v[rows, sl], kh[rows, sl])
    for h, sl in enumerate(heads):
        st_ref[h] = st[h]
        o_h = o[h] + (inter[h][0] if cb == 1 else jnp.concatenate(inter[h], axis=0))
        mix_ref[:, sl] = (_rms(o_h, gn) * gs_ref[:, sl]).astype(BF16)

    @pl.when(blk == pl.num_programs(1) - 1)
    def _():
        for h in range(HG_HEADS):
            sfin_ref[0, h] = st_ref[h].T


def _gla(qs, lf, kk, v, gs, s0, gn, nbatch, cb):
    n, hw = qs.shape
    t = cb * CHUNK
    nblk = n // nbatch // t
    row = pl.BlockSpec((t, hw), lambda b, i: (b * nblk + i, 0))
    st = pl.BlockSpec((1, HG_HEADS, HG_DIM, HG_DIM), lambda b, i: (b, 0, 0, 0))
    return pl.pallas_call(
        functools.partial(_gla_kernel, cb=cb),
        grid=(nbatch, nblk),
        in_specs=[row, row, row, row, row, st, _const_spec(gn.shape)],
        out_specs=[row, st],
        out_shape=[jax.ShapeDtypeStruct((n, hw), BF16),
                   jax.ShapeDtypeStruct(s0.shape, F32)],
        scratch_shapes=[pltpu.VMEM((HG_HEADS, HG_DIM, HG_DIM), F32)],
        compiler_params=_params("arbitrary", "arbitrary"),
        name="gla",
    )(qs, lf, kk, v, gs, s0, gn)


def _rope_tables(tile_pos, tile_offsets):
    half = MLA_ROPE // 2
    inv = jnp.power(ROPE_THETA, -jnp.arange(half, dtype=F32) / half)

    def cos_sin(pos):
        ang = pos.astype(F32)[:, None] * inv[None, :]
        return jnp.cos(ang), jnp.sin(ang)

    pat = lambda a: jnp.concatenate([a, a, jnp.zeros_like(a), jnp.zeros_like(a)], axis=1)
    cos, sin = cos_sin(tile_pos)
    cos0, sin0 = cos_sin(tile_offsets)
    wide = lambda a: jnp.broadcast_to(a[:, :, None], a.shape + (LANES,))
    return (pat(cos), pat(sin), pat(cos0)[:, None, :], pat(sin0)[:, None, :],
            cos.T, sin.T, wide(cos0), wide(sin0))


def _prep_weights(mla_w_in, mla_w_uq, mla_w_uk, mla_w_uv):
    o1 = MLA_Q_RANK + MLA_KV_RANK
    o2 = o1 + MLA_ROPE
    d = mla_w_in.shape[0]
    win = jnp.concatenate([mla_w_in[:, :o1], mla_w_in[:, o2:], mla_w_in[:, o1:o2],
                           jnp.zeros((d, LANES - MLA_ROPE), mla_w_in.dtype)], axis=1)
    wq = mla_w_uq.reshape(MLA_Q_RANK, MLA_HEADS, MLA_NOPE + MLA_ROPE)
    nope = wq[:, :, :MLA_NOPE].reshape(MLA_Q_RANK, MLA_HEADS * MLA_NOPE)
    rope = jnp.pad(wq[:, :, MLA_NOPE:], ((0, 0), (0, 0), (0, LANES - MLA_ROPE)))
    wuq = jnp.concatenate([nope, rope.reshape(MLA_Q_RANK, MLA_HEADS * LANES)], axis=1)
    wuk = mla_w_uk.reshape(MLA_KV_RANK, MLA_HEADS * MLA_NOPE)
    wuv = mla_w_uv.reshape(MLA_KV_RANK, MLA_HEADS * MLA_V)
    wq_t = jnp.pad(wq, ((0, 0), (0, 0), (0, LANES - MLA_ROPE))).reshape(MLA_Q_RANK, -1).T
    bf = lambda a: a.astype(BF16)
    return bf(win), bf(wuq), bf(wuk), bf(wuv), bf(wq_t), bf(wuv.T)


def kernel(x_prompt, x_sample, cache_mla_latent, cache_mla_krope, cache_hgrn_state, cache_mem_k,
           cache_mem_v, mem_prompt, ln_mix_pre, ln_mix_post, ln_ffn_pre, ln_ffn_post, mem_norm,
           w_mem_kv, mla_w_in, mla_q_norm, mla_kv_norm, mla_w_uq, mla_w_uk, mla_w_uv, mla_w_out,
           hgrn_w_in, hgrn_lb, hgrn_o_norm, hgrn_w_out, w_ffn_up, w_ffn_down):
    bp, tp, d = x_prompt.shape
    bs, ts, _ = x_sample.shape
    depth = ln_mix_pre.shape[0]
    past = cache_mla_latent.shape[2]
    n_mem = mem_prompt.shape[1]
    xw = X_HEADS * X_DIM
    assert depth == 2 and bp == 1 and ts == CHUNK and past % CHUNK == 0
    assert MLA_NOPE == LANES and MLA_V == LANES and HG_DIM == LANES and X_DIM == LANES

    row2 = lambda a: a.reshape(1, -1)
    win0, wuq, wuk, wuv, wuq_t, wuv_t = _prep_weights(mla_w_in[0], mla_w_uq[0], mla_w_uk[0],
                                                      mla_w_uv[0])
    wout = (mla_w_out[0].astype(BF16), hgrn_w_out[0].astype(BF16))
    wup = w_ffn_up.astype(BF16)
    wdn = w_ffn_down.astype(BF16)
    win1 = hgrn_w_in[0].astype(BF16)

    mem_k_p, mem_v_p = _memkv(mem_prompt, mem_norm, w_mem_kv.astype(BF16))

    def trunk(x, pos, mem_k, mem_v, s0, nbatch, tm, tb, history):
        n = x.shape[0]
        tables = _rope_tables(*pos)
        pre0_args = (x, row2(ln_mix_pre[0]), win0, row2(mla_q_norm[0]), row2(mla_kv_norm[0]))
        if history is None:
            qt, kn, kr, vt3, lat, krope, xq = _pre0(
                *pre0_args, wuq_t, tables, ATTN_TILE, (wuk, wuv_t))
            mix = _attn_prompt(qt, kn, kr, vt3, ATTN_TILE, ATTN_HEADS_PER_STEP)
        else:
            qn, qr, lat, krope, xq = _pre0(*pre0_args, wuq, tables, tm)
            mix = _attn_sample(qn, qr, history[0], history[1], lat, krope, wuk, wuv, ts)
        x = _post(x, mix, xq, mem_k, mem_v, 0, wout[0],
                  row2(ln_mix_post[0]), row2(ln_ffn_pre[0]), wup, wdn,
                  row2(ln_ffn_post[0]), tm, tb)
        qs, lf, kk, vv, gs, xq = _pre1(x, row2(ln_mix_pre[1]), win1, hgrn_lb, 1, PRE1_TILE)
        mix, s_fin = _gla(qs, lf, kk, vv, gs, s0, row2(hgrn_o_norm[0]), nbatch,
                          4 if n // nbatch >= 4 * CHUNK else 1)
        x = _post(x, mix, xq, mem_k, mem_v, 1, wout[1],
                  row2(ln_mix_post[1]), row2(ln_ffn_pre[1]), wup, wdn,
                  row2(ln_ffn_post[1]), tm, tb)
        return x, lat, krope, s_fin

    s0_p = jnp.zeros((bp, HG_HEADS, HG_DIM, HG_DIM), F32)
    pos_p = (jnp.arange(ATTN_TILE), ATTN_TILE * jnp.arange(bp * tp // ATTN_TILE))
    y_p, lat_p, kr_p, st_p = trunk(x_prompt.reshape(bp * tp, d), pos_p, mem_k_p, mem_v_p,
                                   s0_p, bp, TOKEN_TILE, TOKEN_TILE, None)
    pos_s = (jnp.tile(past + jnp.arange(ts), TOKEN_TILE // ts),
             jnp.zeros((bs * ts // TOKEN_TILE,), jnp.int32))
    mk_s = cache_mem_k.reshape(depth, bs, n_mem * X_HEADS, X_DIM)
    mv_s = cache_mem_v.reshape(depth, bs, n_mem * X_HEADS, X_DIM)
    y_s, lat_s, kr_s, st_s = trunk(x_sample.reshape(bs * ts, d), pos_s, mk_s, mv_s,
                                   cache_hgrn_state[0], bs, TOKEN_TILE, ts,
                                   (cache_mla_latent[0], cache_mla_krope[0]))

    return (y_p.reshape(bp, tp, d), y_s.reshape(bs, ts, d),
            lat_p.reshape(1, bp, tp, MLA_KV_RANK), kr_p.reshape(1, bp, tp, MLA_ROPE),
            st_p.reshape(1, bp, HG_HEADS, HG_DIM, HG_DIM),
            mem_k_p.reshape(depth, bp, n_mem, X_HEADS, X_DIM),
            mem_v_p.reshape(depth, bp, n_mem, X_HEADS, X_DIM),
            lat_s.reshape(1, bs, ts, MLA_KV_RANK), kr_s.reshape(1, bs, ts, MLA_ROPE),
            st_s.reshape(1, bs, HG_HEADS, HG_DIM, HG_DIM))
```

```python
import functools

import jax
import jax.numpy as jnp
from jax import lax
from jax.experimental import pallas as pl
from jax.experimental.pallas import tpu as pltpu

F32 = jnp.float32
BF16 = jnp.bfloat16

CHUNK = 64
EPS = 1e-6
ROPE_THETA = 10000.0
NEG = -1e30
MLA_HEADS = 8
MLA_NOPE = 128
MLA_ROPE = 64
MLA_V = 128
MLA_Q_RANK = 384
MLA_KV_RANK = 256
HG_HEADS = 8
HG_DIM = 128
X_HEADS = 4
X_DIM = 128
LANES = 128
LOG2E = 1.4426950408889634

VMEM_LIMIT_BYTES = 52 * 1024 * 1024

TOKEN_TILE = 512
PRE1_TILE = 256
ATTN_TILE = 512
ATTN_HEADS_PER_STEP = 2
GLA_CHUNKS = 4
SAMPLE_ATTN_BATCH = 4
S_PAD = 128
SUM_ROWS = 16


def _dot(a, b):
    return jnp.dot(a, b, preferred_element_type=F32)


def _dot_nt(a, b):
    return lax.dot_general(a, b, (((1,), (1,)), ((), ())), preferred_element_type=F32)


def _dot_tn(a, b):
    return lax.dot_general(a, b, (((0,), (0,)), ((), ())), preferred_element_type=F32)


def _rms(x, g):
    ms = jnp.mean(x * x, axis=-1, keepdims=True)
    return x * lax.rsqrt(ms + EPS) * g


def _sigmoid(x):
    return 1.0 / (1.0 + jnp.exp(-x))


def _params(*sem):
    return pltpu.CompilerParams(dimension_semantics=sem, vmem_limit_bytes=VMEM_LIMIT_BYTES)


def _const_spec(shape):
    nd = len(shape)
    return pl.BlockSpec(shape, lambda *_: (0,) * nd, pipeline_mode=pl.Buffered(1))


def _layer_spec(shape, layer):
    nd = len(shape) - 1
    return pl.BlockSpec((None,) + tuple(shape[1:]), lambda *_: (layer,) + (0,) * nd,
                        pipeline_mode=pl.Buffered(1))


def _memkv_kernel(mem_ref, g_ref, w_ref, k_ref, v_ref):
    h = _rms(mem_ref[0], g_ref[0]).astype(BF16)
    kv = _dot(h, w_ref[0])
    half = kv.shape[1] // 2
    k_ref[0, 0] = kv[:, :half]
    v_ref[0, 0] = kv[:, half:]


def _memkv(mem, g, w):
    bp, n_mem, d = mem.shape
    depth = g.shape[0]
    width = w.shape[2] // 2
    out = jax.ShapeDtypeStruct((depth, bp, n_mem, width), F32)
    return pl.pallas_call(
        _memkv_kernel,
        grid=(depth, bp),
        in_specs=[
            pl.BlockSpec((1, n_mem, d), lambda l, b: (b, 0, 0)),
            pl.BlockSpec((1, 1, d), lambda l, b: (l, 0, 0)),
            pl.BlockSpec((1, d, 2 * width), lambda l, b: (l, 0, 0)),
        ],
        out_specs=[
            pl.BlockSpec((1, 1, n_mem, width), lambda l, b: (l, b, 0, 0)),
            pl.BlockSpec((1, 1, n_mem, width), lambda l, b: (l, b, 0, 0)),
        ],
        out_shape=[out, out],
        compiler_params=_params("arbitrary", "arbitrary"),
        name="memkv",
    )(mem, g.reshape(depth, 1, d), w)


def _rope_rot(x, c, s1, s2):
    w = x.shape[1]
    half = MLA_ROPE // 2
    return x * c + pltpu.roll(x, half, 1) * s1 + pltpu.roll(x, w - half, 1) * s2


def _pre0_kernel(*refs, expand):
    (x_ref, g_ref, win_ref, qg_ref, kvg_ref, wq_ref,
     cos_ref, sin_ref, cos0_ref, sin0_ref) = refs[:10]
    if expand:
        (wuk_ref, wv_ref, cos_t_ref, sin_t_ref, cos0_t_ref, sin0_t_ref,
         q_ref, kn_ref, kr_ref, v_ref, lat_ref, krope_ref, xq_ref) = refs[10:]
    else:
        qn_ref, qr_ref, lat_ref, krope_ref, xq_ref = refs[10:]
    h = _rms(x_ref[...], g_ref[...]).astype(BF16)
    proj = _dot(h, win_ref[...])
    o1 = MLA_Q_RANK
    o2 = o1 + MLA_KV_RANK
    o3 = o2 + X_HEADS * X_DIM
    c_q, c_kv, xq, k_r = proj[:, :o1], proj[:, o1:o2], proj[:, o2:o3], proj[:, o3:]

    scale = (MLA_NOPE + MLA_ROPE) ** -0.5 * LOG2E
    qc = _rms(c_q, qg_ref[...]).astype(BF16)
    half = MLA_ROPE // 2
    cos0, sin0 = cos0_ref[0], sin0_ref[0]
    c = cos0 * cos_ref[...] - sin0 * sin_ref[...]
    sn = sin0 * cos_ref[...] + cos0 * sin_ref[...]
    lane = lax.broadcasted_iota(jnp.int32, sn.shape, 1)
    s1 = jnp.where(lane >= half, sn, 0.0)
    s2 = jnp.where(lane < half, -sn, 0.0)
    lat = _rms(c_kv, kvg_ref[...])
    lat_ref[...] = lat
    kr = _rope_rot(k_r, c, s1, s2)
    krope_ref[...] = kr[:, :MLA_ROPE]
    xq_ref[...] = (xq * (X_DIM ** -0.5 * LOG2E)).astype(BF16)
    if expand:
        latb = lat.astype(BF16)
        kn_ref[...] = _dot(latb, wuk_ref[...]).astype(BF16)
        kr_ref[...] = kr.astype(BF16)
        qt = _dot_nt(wq_ref[...], qc) * scale
        lanes4 = lambda a: jnp.concatenate([a] * (qt.shape[1] // LANES), axis=1)
        cos0_t, sin0_t = lanes4(cos0_t_ref[0]), lanes4(sin0_t_ref[0])
        cos = cos0_t * cos_t_ref[...] - sin0_t * sin_t_ref[...]
        sin = sin0_t * cos_t_ref[...] + cos0_t * sin_t_ref[...]
        slabs = []
        for hd in range(MLA_HEADS):
            r0 = hd * 2 * LANES + MLA_NOPE
            x1, x2 = qt[r0:r0 + half], qt[r0 + half:r0 + 2 * half]
            slabs += [qt[hd * 2 * LANES:r0], x1 * cos - x2 * sin, x2 * cos + x1 * sin,
                      qt[r0 + 2 * half:(hd + 1) * 2 * LANES]]
        q_ref[...] = jnp.concatenate(slabs, axis=0).astype(BF16)
        v_ref[0] = _dot_nt(wv_ref[...], latb).astype(BF16)
    else:
        q = _dot(qc, wq_ref[...]) * scale
        nope_w = MLA_HEADS * MLA_NOPE
        qn_ref[...] = q[:, :nope_w].astype(BF16)
        tile = lambda t: jnp.concatenate([t] * MLA_HEADS, axis=1)
        qr_ref[...] = _rope_rot(q[:, nope_w:], tile(c), tile(s1), tile(s2)).astype(BF16)


def _pre0(x, g, win, qg, kvg, wq, tables, tm, expand_weights=None):
    n, d = x.shape
    hw = MLA_HEADS * LANES
    xw = X_HEADS * X_DIM
    expand = expand_weights is not None
    row = lambda w: pl.BlockSpec((tm, w), lambda i: (i, 0))
    col = lambda r: pl.BlockSpec((r, tm), lambda i: (0, i))
    sds = jax.ShapeDtypeStruct
    in_specs = [row(d), _const_spec(g.shape), _const_spec(win.shape), _const_spec(qg.shape),
                _const_spec(kvg.shape), _const_spec(wq.shape),
                _const_spec((tm, LANES)), _const_spec((tm, LANES)),
                pl.BlockSpec((1, 1, LANES), lambda i: (i, 0, 0)),
                pl.BlockSpec((1, 1, LANES), lambda i: (i, 0, 0))]
    common_specs = [row(MLA_KV_RANK), row(MLA_ROPE), row(xw)]
    common_shapes = [sds((n, MLA_KV_RANK), F32),
                     sds((n, MLA_ROPE), F32),
                     sds((n, xw), BF16)]
    if expand:
        half = MLA_ROPE // 2
        in_specs += [_const_spec(w.shape) for w in expand_weights]
        in_specs += [_const_spec((half, tm)), _const_spec((half, tm)),
                     pl.BlockSpec((1, half, LANES), lambda i: (i, 0, 0)),
                     pl.BlockSpec((1, half, LANES), lambda i: (i, 0, 0))]
        out_specs = [col(2 * hw), row(hw), row(LANES), pl.BlockSpec((1, hw, tm), lambda i: (i, 0, 0))]
        out_shape = [sds((2 * hw, n), BF16),
                     sds((n, hw), BF16),
                     sds((n, LANES), BF16),
                     sds((n // tm, hw, tm), BF16)]
        operands = (x, g, win, qg, kvg, wq) + tuple(tables[:4]) + tuple(expand_weights) + tuple(tables[4:])
    else:
        out_specs = [row(hw), row(hw)]
        out_shape = [sds((n, hw), BF16),
                     sds((n, hw), BF16)]
        operands = (x, g, win, qg, kvg, wq) + tuple(tables[:4])
    return pl.pallas_call(
        functools.partial(_pre0_kernel, expand=expand),
        grid=(n // tm,),
        in_specs=in_specs,
        out_specs=out_specs + common_specs,
        out_shape=out_shape + common_shapes,
        compiler_params=_params("arbitrary"),
        name="pre0",
    )(*operands)


def _attn_p_kernel(q_ref, kn_ref, kr_ref, v_ref, o_ref, m_ref, acc_ref, sa_ref, sb_ref,
                   mxa_ref, mxb_ref, *, t, hps):
    qi = pl.program_id(1)
    m_ref[...] = jnp.full(m_ref.shape, NEG, F32)
    acc_ref[...] = jnp.zeros(acc_ref.shape, F32)
    ones = jnp.ones((SUM_ROWS, t), BF16)

    def scores(j, s_ref, mx_ref):
        kr = kr_ref[j]
        for hh in range(hps):
            k_blk = jnp.concatenate([kn_ref[j, :, hh * LANES:(hh + 1) * LANES], kr], axis=1)
            s = _dot(k_blk, q_ref[hh * 2 * LANES:(hh + 1) * 2 * LANES, :])
            s_ref[hh, :, :t] = s
            mx_ref[hh] = jnp.max(s, axis=0, keepdims=True)

    def update(j, s_ref, mx_ref, masked):
        if masked:
            keep = (lax.broadcasted_iota(jnp.int32, (t, t), 0) // CHUNK
                    <= lax.broadcasted_iota(jnp.int32, (t, t), 1) // CHUNK)
        for hh in range(hps):
            s = s_ref[hh, :, :t]
            if masked:
                s = jnp.where(keep, s, NEG)
                m_blk = jnp.max(s, axis=0, keepdims=True)
            else:
                m_blk = mx_ref[hh]
            m_prev = m_ref[hh]
            m_new = jnp.maximum(m_prev, m_blk)
            alpha = jnp.exp2(m_prev - m_new)
            p = jnp.exp2(s - m_new).astype(BF16)
            v_aug = jnp.concatenate([v_ref[j, hh * LANES:(hh + 1) * LANES, :], ones], axis=0)
            acc_ref[hh] = alpha * acc_ref[hh] + _dot(v_aug, p)
            m_ref[hh] = m_new

    scores(0, sa_ref, mxa_ref)

    def pair(i, _):
        scores(2 * i + 1, sb_ref, mxb_ref)
        update(2 * i, sa_ref, mxa_ref, False)
        scores(2 * i + 2, sa_ref, mxa_ref)
        update(2 * i + 1, sb_ref, mxb_ref, False)
        return 0

    lax.fori_loop(0, qi // 2, pair, 0)

    @pl.when(qi % 2 == 0)
    def _():
        update(qi, sa_ref, mxa_ref, True)

    @pl.when(qi % 2 == 1)
    def _():
        scores(qi, sb_ref, mxb_ref)
        update(qi - 1, sa_ref, mxa_ref, False)
        update(qi, sb_ref, mxb_ref, True)

    for hh in range(hps):
        o = acc_ref[hh, :LANES, :] / acc_ref[hh, LANES:LANES + 1, :]
        o_ref[:, hh * LANES:(hh + 1) * LANES] = o.T.astype(BF16)


def _attn_prompt(qt, kn, kr, vt3, t, hps):
    n, hw = kn.shape
    nb = n // t
    kn3 = kn.reshape(nb, t, hw)
    kr3 = kr.reshape(nb, t, LANES)
    return pl.pallas_call(
        functools.partial(_attn_p_kernel, t=t, hps=hps),
        grid=(MLA_HEADS // hps, nb),
        in_specs=[
            pl.BlockSpec((hps * 2 * LANES, t), lambda h, i: (h, i)),
            pl.BlockSpec((nb, t, hps * LANES), lambda h, i: (0, 0, h), pipeline_mode=pl.Buffered(1)),
            pl.BlockSpec((nb, t, LANES), lambda h, i: (0, 0, 0), pipeline_mode=pl.Buffered(1)),
            pl.BlockSpec((nb, hps * LANES, t), lambda h, i: (0, h, 0), pipeline_mode=pl.Buffered(1)),
        ],
        out_specs=pl.BlockSpec((t, hps * LANES), lambda h, i: (i, h)),
        out_shape=jax.ShapeDtypeStruct((n, hw), BF16),
        scratch_shapes=[pltpu.VMEM((hps, 1, t), F32),
                        pltpu.VMEM((hps, LANES + SUM_ROWS, t), F32),
                        pltpu.VMEM((hps, t, t + S_PAD), F32), pltpu.VMEM((hps, t, t + S_PAD), F32),
                        pltpu.VMEM((hps, 1, t), F32), pltpu.VMEM((hps, 1, t), F32)],
        compiler_params=_params("arbitrary", "arbitrary"),
        name="attn_prompt",
    )(qt, kn3, kr3, vt3)


def _attn_s_kernel(qn_ref, qr_ref, latp_ref, krp_ref, latn_ref, krn_ref, wuk_ref, wuv_ref, o_ref,
                   *, past, ts, bps):
    rows = [slice(e * ts, (e + 1) * ts) for e in range(bps)]
    heads = [slice(h * LANES, (h + 1) * LANES) for h in range(MLA_HEADS)]
    lat = [jnp.concatenate([latp_ref[e].astype(BF16), latn_ref[r, :].astype(BF16)], axis=0)
           for e, r in enumerate(rows)]
    kr = [jnp.concatenate([krp_ref[e].astype(BF16), krn_ref[r, :].astype(BF16)], axis=0)
          for e, r in enumerate(rows)]
    q_lat_h = [_dot_nt(qn_ref[:, sl], wuk_ref[:, sl]).astype(BF16) for sl in heads]
    q_lat = [jnp.concatenate([qh[r] for qh in q_lat_h], axis=0) for r in rows]
    q_rope = [jnp.concatenate([qr_ref[r, sl.start:sl.start + MLA_ROPE] for sl in heads], axis=0)
              for r in rows]
    s = [_dot_nt(q_lat[e], lat[e]) + _dot_nt(q_rope[e], kr[e]) for e in range(bps)]
    if (past + ts - 1) // CHUNK > past // CHUNK:
        n_keys = past + ts
        qrow = lax.broadcasted_iota(jnp.int32, (MLA_HEADS * ts, n_keys), 0)
        keys = lax.broadcasted_iota(jnp.int32, (MLA_HEADS * ts, n_keys), 1)
        keep = keys // CHUNK <= (past + qrow % ts) // CHUNK
        s = [jnp.where(keep, si, NEG) for si in s]
    p = [jnp.exp2(si - jnp.max(si, axis=1, keepdims=True)) for si in s]
    l = [jnp.sum(pi, axis=1, keepdims=True) for pi in p]
    o_lat = [_dot(pi.astype(BF16), la) for pi, la in zip(p, lat)]
    o_lat = [(oi / li).astype(BF16) for oi, li in zip(o_lat, l)]
    for h, sl in enumerate(heads):
        o_h = jnp.concatenate([oi[h * ts:(h + 1) * ts] for oi in o_lat], axis=0)
        o_ref[:, sl] = _dot(o_h, wuv_ref[:, sl]).astype(BF16)


def _attn_sample(qn, qr, lat_past, kr_past, lat_new, kr_new, wuk, wuv, ts, bps):
    n, hw = qn.shape
    nb, past, _ = lat_past.shape
    new = lambda w: pl.BlockSpec((bps * ts, w), lambda b: (b, 0))
    old = lambda w: pl.BlockSpec((bps, past, w), lambda b: (b, 0, 0))
    return pl.pallas_call(
        functools.partial(_attn_s_kernel, past=past, ts=ts, bps=bps),
        grid=(nb // bps,),
        in_specs=[new(hw), new(hw), old(MLA_KV_RANK), old(MLA_ROPE), new(MLA_KV_RANK),
                  new(MLA_ROPE), _const_spec(wuk.shape), _const_spec(wuv.shape)],
        out_specs=new(hw),
        out_shape=jax.ShapeDtypeStruct((n, hw), BF16),
        compiler_params=_params("arbitrary"),
        name="attn_sample",
    )(qn, qr, lat_past, kr_past, lat_new, kr_new, wuk, wuv)


def _post_kernel(x_ref, mix_ref, xq_ref, mk_ref, mv_ref, wo_ref, g1_ref, g2_ref, wup_ref,
                 wdn_ref, g3_ref, y_ref, *, nb, tb, interleaved):
    def mem_head(ref, b, h):
        if interleaved:
            return ref[b, pl.ds(h, ref.shape[1] // X_HEADS, stride=X_HEADS), :].astype(BF16)
        return ref[b, :, h * X_DIM:(h + 1) * X_DIM].astype(BF16)

    pairs = [(b, h) for b in range(nb) for h in range(X_HEADS)]
    s = [_dot_nt(xq_ref[b * tb:(b + 1) * tb, h * X_DIM:(h + 1) * X_DIM], mem_head(mk_ref, b, h))
         for b, h in pairs]
    p = [jnp.exp2(si - jnp.max(si, axis=1, keepdims=True)) for si in s]
    l = [jnp.sum(pi, axis=1, keepdims=True) for pi in p]
    o = [_dot(pi.astype(BF16), mem_head(mv_ref, b, h)) for pi, (b, h) in zip(p, pairs)]
    o = [(oi / li).astype(BF16) for oi, li in zip(o, l)]
    cross = [jnp.concatenate(o[b * X_HEADS:(b + 1) * X_HEADS], axis=1) for b in range(nb)]
    cross = cross[0] if nb == 1 else jnp.concatenate(cross, axis=0)
    cat = jnp.concatenate([mix_ref[...], cross], axis=1)
    x1 = x_ref[...] + _rms(_dot(cat, wo_ref[...]), g1_ref[...])

    hb = _rms(x1, g2_ref[...]).astype(BF16)
    d_ff = wup_ref.shape[1]
    step = 1024
    acc = None
    for c in range(d_ff // step):
        up = jnp.maximum(_dot(hb, wup_ref[:, c * step:(c + 1) * step]), 0.0)
        dn = _dot((up * up).astype(BF16), wdn_ref[c * step:(c + 1) * step, :])
        acc = dn if acc is None else acc + dn
    y_ref[...] = x1 + _rms(acc, g3_ref[...])


def _post(x, mix, xq, mk, mv, layer, wo, g1, g2, wup, wdn, g3, tm, tb):
    n, d = x.shape
    nb = tm // tb
    xw = X_HEADS * X_DIM
    _, nmem_b, mrows, mwidth = mk.shape
    interleaved = mwidth == X_DIM
    row = lambda w: pl.BlockSpec((tm, w), lambda i: (i, 0))
    if nmem_b * tb == n:
        mem = pl.BlockSpec((None, nb, mrows, mwidth), lambda i: (layer, i, 0, 0))
    else:
        assert nmem_b == 1 and nb == 1
        mem = pl.BlockSpec((None, 1, mrows, mwidth), lambda i: (layer, 0, 0, 0))
    return pl.pallas_call(
        functools.partial(_post_kernel, nb=nb, tb=tb, interleaved=interleaved),
        grid=(n // tm,),
        in_specs=[row(d), row(mix.shape[1]), row(xw), mem, mem, _const_spec(wo.shape),
                  _const_spec(g1.shape), _const_spec(g2.shape), _layer_spec(wup.shape, layer),
                  _layer_spec(wdn.shape, layer), _const_spec(g3.shape)],
        out_specs=row(d),
        out_shape=jax.ShapeDtypeStruct((n, d), F32),
        compiler_params=_params("arbitrary"),
        name="post",
    )(x, mix, xq, mk, mv, wo, g1, g2, wup, wdn, g3)


def _pre1_kernel(x_ref, g_ref, w_ref, lb_ref,
                 qs_ref, lf_ref, kk_ref, v_ref, gs_ref, xq_ref, *, layer):
    hb = _rms(x_ref[...], g_ref[...]).astype(BF16)
    w = HG_HEADS * HG_DIM
    lb_all = lb_ref[...]
    e = jnp.exp(lb_all - jnp.max(lb_all, axis=0, keepdims=True))
    soft = e / jnp.sum(e, axis=0, keepdims=True)
    lb = jnp.sum(soft[:layer + 1], axis=0, keepdims=True) - soft[0:1]
    q = _dot(hb, w_ref[:, :w])
    qs_ref[...] = q * _sigmoid(q)
    f = _dot(hb, w_ref[:, w:2 * w])
    lf_ref[...] = jnp.log(lb + (1.0 - lb) * _sigmoid(f))
    kk_ref[...] = (1.0 - lb) * _sigmoid(-f)
    v_ref[...] = _dot(hb, w_ref[:, 2 * w:3 * w]).astype(BF16)
    g = _dot(hb, w_ref[:, 3 * w:4 * w])
    gs_ref[...] = g * _sigmoid(g)
    xq_ref[...] = (_dot(hb, w_ref[:, 4 * w:]) * (X_DIM ** -0.5 * LOG2E)).astype(BF16)


def _pre1(x, g, w, lb, layer, tm):
    n, d = x.shape
    hw = HG_HEADS * HG_DIM
    xw = X_HEADS * X_DIM
    row = lambda wd: pl.BlockSpec((tm, wd), lambda i: (i, 0))
    f32o = jax.ShapeDtypeStruct((n, hw), F32)
    return pl.pallas_call(
        functools.partial(_pre1_kernel, layer=layer),
        grid=(n // tm,),
        in_specs=[row(d), _const_spec(g.shape), _const_spec(w.shape), _const_spec(lb.shape)],
        out_specs=[row(hw), row(hw), row(hw), row(hw), row(hw), row(xw)],
        out_shape=[f32o, f32o, f32o, jax.ShapeDtypeStruct((n, hw), BF16), f32o,
                   jax.ShapeDtypeStruct((n, xw), BF16)],
        compiler_params=_params("arbitrary"),
        name="pre1",
    )(x, g, w, lb)


def _split3(x):
    hi = x.astype(BF16)
    r1 = x - hi.astype(F32)
    mid = r1.astype(BF16)
    lo = (r1 - mid.astype(F32)).astype(BF16)
    return hi, mid, lo


def _gla_kernel(qs_ref, lf_ref, kk_ref, v_ref, gs_ref, s0_ref, gn_ref, mix_ref, sfin_ref,
                st_ref, *, cb, independent):
    blk = pl.program_id(1)
    t = cb * CHUNK
    d = HG_DIM

    if not independent:
        @pl.when(blk == 0)
        def _():
            for h in range(HG_HEADS):
                st_ref[h] = s0_ref[0, h].T

    row = lax.broadcasted_iota(jnp.int32, (t, t), 0)
    col = lax.broadcasted_iota(jnp.int32, (t, t), 1)
    causal = jnp.logical_and(row // CHUNK == col // CHUNK, col <= row)
    tri = jnp.where(causal, 1.0, 0.0).astype(BF16)
    gn = gn_ref[...]
    hw = HG_HEADS * d
    heads = [slice(h * d, (h + 1) * d) for h in range(HG_HEADS)]

    bb = _dot(tri, jnp.concatenate(_split3(lf_ref[...]), axis=1))
    b = (bb[:, :hw] + bb[:, hw:2 * hw]) + bb[:, 2 * hw:]
    bcast = lambda r: jnp.concatenate(
        [jnp.broadcast_to(b[c * CHUNK + r:c * CHUNK + r + 1, :], (CHUNK, hw)) for c in range(cb)],
        axis=0)
    ref = bcast(CHUNK // 2)
    last = bcast(CHUNK - 1)
    qs = qs_ref[...]
    kk = kk_ref[...]
    v = v_ref[...]
    qe = (qs * jnp.exp(b - ref)).astype(BF16)
    ke = (kk * jnp.exp(ref - b)).astype(BF16)
    kh = (kk * jnp.exp(last - b)).astype(BF16)
    qb = (qs * jnp.exp(b)).astype(BF16)
    a = [jnp.where(causal, _dot_nt(qe[:, sl], ke[:, sl]), 0.0).astype(BF16) for sl in heads]
    o = [_dot(a[h], v[:, sl]) for h, sl in enumerate(heads)]
    if not independent:
        st = [st_ref[h] for h in range(HG_HEADS)]
    inter = [[] for _ in heads]
    for c in range(cb):
        rows = slice(c * CHUNK, (c + 1) * CHUNK)
        decay = jnp.exp(b[c * CHUNK + CHUNK - 1:c * CHUNK + CHUNK, :])
        if independent:
            st = [s0_ref[c, h].T for h in range(HG_HEADS)]
        for h, sl in enumerate(heads):
            inter[h].append(_dot_nt(qb[rows, sl], st[h].astype(BF16)))
        for h, sl in enumerate(heads):
            st[h] = st[h] * decay[:, sl] + _dot_tn(v[rows, sl], kh[rows, sl])
        if independent:
            for h in range(HG_HEADS):
                sfin_ref[c, h] = st[h].T
    for h, sl in enumerate(heads):
        o_h = o[h] + (inter[h][0] if cb == 1 else jnp.concatenate(inter[h], axis=0))
        mix_ref[:, sl] = (_rms(o_h, gn) * gs_ref[:, sl]).astype(BF16)

    if not independent:
        for h in range(HG_HEADS):
            st_ref[h] = st[h]

        @pl.when(blk == pl.num_programs(1) - 1)
        def _():
            for h in range(HG_HEADS):
                sfin_ref[0, h] = st_ref[h].T


def _gla(qs, lf, kk, v, gs, s0, gn, nbatch, cb):
    n, hw = qs.shape
    t = cb * CHUNK
    independent = n // nbatch == CHUNK
    if independent:
        nbatch, nblk, sb = nbatch // cb, 1, cb
    else:
        nblk, sb = n // nbatch // t, 1
    row = pl.BlockSpec((t, hw), lambda b, i: (b * nblk + i, 0))
    st = pl.BlockSpec((sb, HG_HEADS, HG_DIM, HG_DIM), lambda b, i: (b, 0, 0, 0))
    return pl.pallas_call(
        functools.partial(_gla_kernel, cb=cb, independent=independent),
        grid=(nbatch, nblk),
        in_specs=[row, row, row, row, row, st, _const_spec(gn.shape)],
        out_specs=[row, st],
        out_shape=[jax.ShapeDtypeStruct((n, hw), BF16),
                   jax.ShapeDtypeStruct(s0.shape, F32)],
        scratch_shapes=[pltpu.VMEM((HG_HEADS, HG_DIM, HG_DIM), F32)],
        compiler_params=_params("arbitrary", "arbitrary"),
        name="gla",
    )(qs, lf, kk, v, gs, s0, gn)


def _rope_tables(tile_pos, tile_offsets):
    half = MLA_ROPE // 2
    inv = jnp.power(ROPE_THETA, -jnp.arange(half, dtype=F32) / half)

    def cos_sin(pos):
        ang = pos.astype(F32)[:, None] * inv[None, :]
        return jnp.cos(ang), jnp.sin(ang)

    pat = lambda a: jnp.concatenate([a, a, jnp.zeros_like(a), jnp.zeros_like(a)], axis=1)
    cos, sin = cos_sin(tile_pos)
    cos0, sin0 = cos_sin(tile_offsets)
    wide = lambda a: jnp.broadcast_to(a[:, :, None], a.shape + (LANES,))
    return (pat(cos), pat(sin), pat(cos0)[:, None, :], pat(sin0)[:, None, :],
            cos.T, sin.T, wide(cos0), wide(sin0))


def _prep_weights(mla_w_in, mla_w_uq, mla_w_uk, mla_w_uv):
    o1 = MLA_Q_RANK + MLA_KV_RANK
    o2 = o1 + MLA_ROPE
    d = mla_w_in.shape[0]
    win = jnp.concatenate([mla_w_in[:, :o1], mla_w_in[:, o2:], mla_w_in[:, o1:o2],
                           jnp.zeros((d, LANES - MLA_ROPE), mla_w_in.dtype)], axis=1)
    wq = mla_w_uq.reshape(MLA_Q_RANK, MLA_HEADS, MLA_NOPE + MLA_ROPE)
    nope = wq[:, :, :MLA_NOPE].reshape(MLA_Q_RANK, MLA_HEADS * MLA_NOPE)
    rope = jnp.pad(wq[:, :, MLA_NOPE:], ((0, 0), (0, 0), (0, LANES - MLA_ROPE)))
    wuq = jnp.concatenate([nope, rope.reshape(MLA_Q_RANK, MLA_HEADS * LANES)], axis=1)
    wuk = mla_w_uk.reshape(MLA_KV_RANK, MLA_HEADS * MLA_NOPE)
    wuv = mla_w_uv.reshape(MLA_KV_RANK, MLA_HEADS * MLA_V)
    wq_t = jnp.pad(wq, ((0, 0), (0, 0), (0, LANES - MLA_ROPE))).reshape(MLA_Q_RANK, -1).T
    bf = lambda a: a.astype(BF16)
    return bf(win), bf(wuq), bf(wuk), bf(wuv), bf(wq_t), bf(wuv.T)


def kernel(x_prompt, x_sample, cache_mla_latent, cache_mla_krope, cache_hgrn_state, cache_mem_k,
           cache_mem_v, mem_prompt, ln_mix_pre, ln_mix_post, ln_ffn_pre, ln_ffn_post, mem_norm,
           w_mem_kv, mla_w_in, mla_q_norm, mla_kv_norm, mla_w_uq, mla_w_uk, mla_w_uv, mla_w_out,
           hgrn_w_in, hgrn_lb, hgrn_o_norm, hgrn_w_out, w_ffn_up, w_ffn_down):
    bp, tp, d = x_prompt.shape
    bs, ts, _ = x_sample.shape
    depth = ln_mix_pre.shape[0]
    past = cache_mla_latent.shape[2]
    n_mem = mem_prompt.shape[1]
    xw = X_HEADS * X_DIM
    assert depth == 2 and bp == 1 and ts == CHUNK and past % CHUNK == 0
    assert MLA_NOPE == LANES and MLA_V == LANES and HG_DIM == LANES and X_DIM == LANES

    row2 = lambda a: a.reshape(1, -1)
    win0, wuq, wuk, wuv, wuq_t, wuv_t = _prep_weights(mla_w_in[0], mla_w_uq[0], mla_w_uk[0],
                                                      mla_w_uv[0])
    wout = (mla_w_out[0].astype(BF16), hgrn_w_out[0].astype(BF16))
    wup = w_ffn_up.astype(BF16)
    wdn = w_ffn_down.astype(BF16)
    win1 = hgrn_w_in[0].astype(BF16)

    mem_k_p, mem_v_p = _memkv(mem_prompt, mem_norm, w_mem_kv.astype(BF16))

    def trunk(x, pos, mem_k, mem_v, s0, nbatch, tm, tb, history):
        n = x.shape[0]
        tables = _rope_tables(*pos)
        pre0_args = (x, row2(ln_mix_pre[0]), win0, row2(mla_q_norm[0]), row2(mla_kv_norm[0]))
        if history is None:
            qt, kn, kr, vt3, lat, krope, xq = _pre0(
                *pre0_args, wuq_t, tables, ATTN_TILE, (wuk, wuv_t))
            mix = _attn_prompt(qt, kn, kr, vt3, ATTN_TILE, ATTN_HEADS_PER_STEP)
        else:
            qn, qr, lat, krope, xq = _pre0(*pre0_args, wuq, tables, tm)
            mix = _attn_sample(qn, qr, history[0], history[1], lat, krope, wuk, wuv, ts,
                               SAMPLE_ATTN_BATCH)
        x = _post(x, mix, xq, mem_k, mem_v, 0, wout[0],
                  row2(ln_mix_post[0]), row2(ln_ffn_pre[0]), wup, wdn,
                  row2(ln_ffn_post[0]), tm, tb)
        qs, lf, kk, vv, gs, xq = _pre1(x, row2(ln_mix_pre[1]), win1, hgrn_lb, 1, PRE1_TILE)
        mix, s_fin = _gla(qs, lf, kk, vv, gs, s0, row2(hgrn_o_norm[0]), nbatch, GLA_CHUNKS)
        x = _post(x, mix, xq, mem_k, mem_v, 1, wout[1],
                  row2(ln_mix_post[1]), row2(ln_ffn_pre[1]), wup, wdn,
                  row2(ln_ffn_post[1]), tm, tb)
        return x, lat, krope, s_fin

    s0_p = jnp.zeros((bp, HG_HEADS, HG_DIM, HG_DIM), F32)
    pos_p = (jnp.arange(ATTN_TILE), ATTN_TILE * jnp.arange(bp * tp // ATTN_TILE))
    y_p, lat_p, kr_p, st_p = trunk(x_prompt.reshape(bp * tp, d), pos_p, mem_k_p, mem_v_p,
                                   s0_p, bp, TOKEN_TILE, TOKEN_TILE, None)
    pos_s = (jnp.tile(past + jnp.arange(ts), TOKEN_TILE // ts),
             jnp.zeros((bs * ts // TOKEN_TILE,), jnp.int32))
    mk_s = cache_mem_k.reshape(depth, bs, n_mem * X_HEADS, X_DIM)
    mv_s = cache_mem_v.reshape(depth, bs, n_mem * X_HEADS, X_DIM)
    y_s, lat_s, kr_s, st_s = trunk(x_sample.reshape(bs * ts, d), pos_s, mk_s, mv_s,
                                   cache_hgrn_state[0], bs, TOKEN_TILE, ts,
                                   (cache_mla_latent[0], cache_mla_krope[0]))

    return (y_p.reshape(bp, tp, d), y_s.reshape(bs, ts, d),
            lat_p.reshape(1, bp, tp, MLA_KV_RANK), kr_p.reshape(1, bp, tp, MLA_ROPE),
            st_p.reshape(1, bp, HG_HEADS, HG_DIM, HG_DIM),
            mem_k_p.reshape(depth, bp, n_mem, X_HEADS, X_DIM),
            mem_v_p.reshape(depth, bp, n_mem, X_HEADS, X_DIM),
            lat_s.reshape(1, bs, ts, MLA_KV_RANK), kr_s.reshape(1, bs, ts, MLA_ROPE),
            st_s.reshape(1, bs, HG_HEADS, HG_DIM, HG_DIM))
```

```python
import functools

import jax
import jax.numpy as jnp
from jax import lax
from jax.experimental import pallas as pl
from jax.experimental.pallas import tpu as pltpu

F32 = jnp.float32
BF16 = jnp.bfloat16

CHUNK = 64
EPS = 1e-6
ROPE_THETA = 10000.0
NEG = -1e30
MLA_HEADS = 8
MLA_NOPE = 128
MLA_ROPE = 64
MLA_V = 128
MLA_Q_RANK = 384
MLA_KV_RANK = 256
HG_HEADS = 8
HG_DIM = 128
X_HEADS = 4
X_DIM = 128
LANES = 128
LOG2E = 1.4426950408889634

VMEM_LIMIT_BYTES = 52 * 1024 * 1024

TOKEN_TILE = 512
PRE1_TILE = 256
ATTN_TILE = 512
ATTN_HEADS_PER_STEP = 2
GLA_CHUNKS = 4
SAMPLE_ATTN_BATCH = 4
S_PAD = 128
SUM_ROWS = 16


def _dot(a, b):
    return jnp.dot(a, b, preferred_element_type=F32)


def _dot_nt(a, b):
    return lax.dot_general(a, b, (((1,), (1,)), ((), ())), preferred_element_type=F32)


def _dot_tn(a, b):
    return lax.dot_general(a, b, (((0,), (0,)), ((), ())), preferred_element_type=F32)


def _rms(x, g):
    ms = jnp.mean(x * x, axis=-1, keepdims=True)
    return x * lax.rsqrt(ms + EPS) * g


def _sigmoid(x):
    return 1.0 / (1.0 + jnp.exp(-x))


def _params(*sem):
    return pltpu.CompilerParams(dimension_semantics=sem, vmem_limit_bytes=VMEM_LIMIT_BYTES)


def _const_spec(shape):
    nd = len(shape)
    return pl.BlockSpec(shape, lambda *_: (0,) * nd, pipeline_mode=pl.Buffered(1))


def _layer_spec(shape, layer):
    nd = len(shape) - 1
    return pl.BlockSpec((None,) + tuple(shape[1:]), lambda *_: (layer,) + (0,) * nd,
                        pipeline_mode=pl.Buffered(1))


def _memkv_kernel(mem_ref, g_ref, w_ref, k_ref, v_ref):
    h = _rms(mem_ref[0], g_ref[0]).astype(BF16)
    kv = _dot(h, w_ref[0])
    half = kv.shape[1] // 2
    k_ref[0, 0] = kv[:, :half]
    v_ref[0, 0] = kv[:, half:]


def _memkv(mem, g, w):
    bp, n_mem, d = mem.shape
    depth = g.shape[0]
    width = w.shape[2] // 2
    out = jax.ShapeDtypeStruct((depth, bp, n_mem, width), F32)
    return pl.pallas_call(
        _memkv_kernel,
        grid=(depth, bp),
        in_specs=[
            pl.BlockSpec((1, n_mem, d), lambda l, b: (b, 0, 0)),
            pl.BlockSpec((1, 1, d), lambda l, b: (l, 0, 0)),
            pl.BlockSpec((1, d, 2 * width), lambda l, b: (l, 0, 0)),
        ],
        out_specs=[
            pl.BlockSpec((1, 1, n_mem, width), lambda l, b: (l, b, 0, 0)),
            pl.BlockSpec((1, 1, n_mem, width), lambda l, b: (l, b, 0, 0)),
        ],
        out_shape=[out, out],
        compiler_params=_params("arbitrary", "arbitrary"),
        name="memkv",
    )(mem, g.reshape(depth, 1, d), w)


def _rope_rot(x, c, s1, s2):
    w = x.shape[1]
    half = MLA_ROPE // 2
    return x * c + pltpu.roll(x, half, 1) * s1 + pltpu.roll(x, w - half, 1) * s2


def _pre0_kernel(*refs, expand):
    (x_ref, g_ref, win_ref, qg_ref, kvg_ref, wq_ref,
     cos_ref, sin_ref, cos0_ref, sin0_ref) = refs[:10]
    if expand:
        (wuk_ref, wv_ref, cos_t_ref, sin_t_ref, cos0_t_ref, sin0_t_ref,
         q_ref, kn_ref, kr_ref, v_ref, lat_ref, krope_ref, xq_ref) = refs[10:]
    else:
        qn_ref, qr_ref, lat_ref, krope_ref, xq_ref = refs[10:]
    h = _rms(x_ref[...], g_ref[...]).astype(BF16)
    proj = _dot(h, win_ref[...])
    o1 = MLA_Q_RANK
    o2 = o1 + MLA_KV_RANK
    o3 = o2 + X_HEADS * X_DIM
    c_q, c_kv, xq, k_r = proj[:, :o1], proj[:, o1:o2], proj[:, o2:o3], proj[:, o3:]

    scale = (MLA_NOPE + MLA_ROPE) ** -0.5 * LOG2E
    qc = _rms(c_q, qg_ref[...]).astype(BF16)
    half = MLA_ROPE // 2
    cos0, sin0 = cos0_ref[0], sin0_ref[0]
    c = cos0 * cos_ref[...] - sin0 * sin_ref[...]
    sn = sin0 * cos_ref[...] + cos0 * sin_ref[...]
    lane = lax.broadcasted_iota(jnp.int32, sn.shape, 1)
    s1 = jnp.where(lane >= half, sn, 0.0)
    s2 = jnp.where(lane < half, -sn, 0.0)
    lat = _rms(c_kv, kvg_ref[...])
    lat_ref[...] = lat
    kr = _rope_rot(k_r, c, s1, s2)
    krope_ref[...] = kr[:, :MLA_ROPE]
    xq_ref[...] = (xq * (X_DIM ** -0.5 * LOG2E)).astype(BF16)
    if expand:
        latb = lat.astype(BF16)
        kn_ref[...] = _dot(latb, wuk_ref[...]).astype(BF16)
        kr_ref[...] = kr.astype(BF16)
        qt = _dot_nt(wq_ref[...], qc) * scale
        lanes4 = lambda a: jnp.concatenate([a] * (qt.shape[1] // LANES), axis=1)
        cos0_t, sin0_t = lanes4(cos0_t_ref[0]), lanes4(sin0_t_ref[0])
        cos = cos0_t * cos_t_ref[...] - sin0_t * sin_t_ref[...]
        sin = sin0_t * cos_t_ref[...] + cos0_t * sin_t_ref[...]
        slabs = []
        for hd in range(MLA_HEADS):
            r0 = hd * 2 * LANES + MLA_NOPE
            x1, x2 = qt[r0:r0 + half], qt[r0 + half:r0 + 2 * half]
            slabs += [qt[hd * 2 * LANES:r0], x1 * cos - x2 * sin, x2 * cos + x1 * sin,
                      qt[r0 + 2 * half:(hd + 1) * 2 * LANES]]
        q_ref[...] = jnp.concatenate(slabs, axis=0).astype(BF16)
        v_ref[0] = _dot_nt(wv_ref[...], latb).astype(BF16)
    else:
        q = _dot(qc, wq_ref[...]) * scale
        nope_w = MLA_HEADS * MLA_NOPE
        qn_ref[...] = q[:, :nope_w].astype(BF16)
        tile = lambda t: jnp.concatenate([t] * MLA_HEADS, axis=1)
        qr_ref[...] = _rope_rot(q[:, nope_w:], tile(c), tile(s1), tile(s2)).astype(BF16)


def _pre0(x, g, win, qg, kvg, wq, tables, tm, expand_weights=None):
    n, d = x.shape
    hw = MLA_HEADS * LANES
    xw = X_HEADS * X_DIM
    expand = expand_weights is not None
    row = lambda w: pl.BlockSpec((tm, w), lambda i: (i, 0))
    col = lambda r: pl.BlockSpec((r, tm), lambda i: (0, i))
    sds = jax.ShapeDtypeStruct
    in_specs = [row(d), _const_spec(g.shape), _const_spec(win.shape), _const_spec(qg.shape),
                _const_spec(kvg.shape), _const_spec(wq.shape),
                _const_spec((tm, LANES)), _const_spec((tm, LANES)),
                pl.BlockSpec((1, 1, LANES), lambda i: (i, 0, 0)),
                pl.BlockSpec((1, 1, LANES), lambda i: (i, 0, 0))]
    common_specs = [row(MLA_KV_RANK), row(MLA_ROPE), row(xw)]
    common_shapes = [sds((n, MLA_KV_RANK), F32),
                     sds((n, MLA_ROPE), F32),
                     sds((n, xw), BF16)]
    if expand:
        half = MLA_ROPE // 2
        in_specs += [_const_spec(w.shape) for w in expand_weights]
        in_specs += [_const_spec((half, tm)), _const_spec((half, tm)),
                     pl.BlockSpec((1, half, LANES), lambda i: (i, 0, 0)),
                     pl.BlockSpec((1, half, LANES), lambda i: (i, 0, 0))]
        out_specs = [col(2 * hw), row(hw), row(LANES), pl.BlockSpec((1, hw, tm), lambda i: (i, 0, 0))]
        out_shape = [sds((2 * hw, n), BF16),
                     sds((n, hw), BF16),
                     sds((n, LANES), BF16),
                     sds((n // tm, hw, tm), BF16)]
        operands = (x, g, win, qg, kvg, wq) + tuple(tables[:4]) + tuple(expand_weights) + tuple(tables[4:])
    else:
        out_specs = [row(hw), row(hw)]
        out_shape = [sds((n, hw), BF16),
                     sds((n, hw), BF16)]
        operands = (x, g, win, qg, kvg, wq) + tuple(tables[:4])
    return pl.pallas_call(
        functools.partial(_pre0_kernel, expand=expand),
        grid=(n // tm,),
        in_specs=in_specs,
        out_specs=out_specs + common_specs,
        out_shape=out_shape + common_shapes,
        compiler_params=_params("arbitrary"),
        name="pre0",
    )(*operands)


def _attn_p_kernel(q_ref, kn_ref, kr_ref, v_ref, o_ref, m_ref, acc_ref, sa_ref, sb_ref,
                   mxa_ref, mxb_ref, *, t, hps):
    qi = pl.program_id(1)
    m_ref[...] = jnp.full(m_ref.shape, NEG, F32)
    acc_ref[...] = jnp.zeros(acc_ref.shape, F32)
    ones = jnp.ones((SUM_ROWS, t), BF16)

    def scores(j, s_ref, mx_ref):
        kr = kr_ref[j]
        for hh in range(hps):
            k_blk = jnp.concatenate([kn_ref[j, :, hh * LANES:(hh + 1) * LANES], kr], axis=1)
            s = _dot(k_blk, q_ref[hh * 2 * LANES:(hh + 1) * 2 * LANES, :])
            s_ref[hh, :, :t] = s
            mx_ref[hh] = jnp.max(s, axis=0, keepdims=True)

    def update(j, s_ref, mx_ref, masked):
        if masked:
            keep = (lax.broadcasted_iota(jnp.int32, (t, t), 0) // CHUNK
                    <= lax.broadcasted_iota(jnp.int32, (t, t), 1) // CHUNK)
        for hh in range(hps):
            s = s_ref[hh, :, :t]
            if masked:
                s = jnp.where(keep, s, NEG)
                m_blk = jnp.max(s, axis=0, keepdims=True)
            else:
                m_blk = mx_ref[hh]
            m_prev = m_ref[hh]
            m_new = jnp.maximum(m_prev, m_blk)
            alpha = jnp.exp2(m_prev - m_new)
            p = jnp.exp2(s - m_new).astype(BF16)
            v_aug = jnp.concatenate([v_ref[j, hh * LANES:(hh + 1) * LANES, :], ones], axis=0)
            acc_ref[hh] = alpha * acc_ref[hh] + _dot(v_aug, p)
            m_ref[hh] = m_new

    scores(0, sa_ref, mxa_ref)

    def pair(i, _):
        scores(2 * i + 1, sb_ref, mxb_ref)
        update(2 * i, sa_ref, mxa_ref, False)
        scores(2 * i + 2, sa_ref, mxa_ref)
        update(2 * i + 1, sb_ref, mxb_ref, False)
        return 0

    lax.fori_loop(0, qi // 2, pair, 0)

    @pl.when(qi % 2 == 0)
    def _():
        update(qi, sa_ref, mxa_ref, True)

    @pl.when(qi % 2 == 1)
    def _():
        scores(qi, sb_ref, mxb_ref)
        update(qi - 1, sa_ref, mxa_ref, False)
        update(qi, sb_ref, mxb_ref, True)

    for hh in range(hps):
        o = acc_ref[hh, :LANES, :] / acc_ref[hh, LANES:LANES + 1, :]
        o_ref[:, hh * LANES:(hh + 1) * LANES] = o.T.astype(BF16)


def _attn_prompt(qt, kn, kr, vt3, t, hps):
    n, hw = kn.shape
    nb = n // t
    kn3 = kn.reshape(nb, t, hw)
    kr3 = kr.reshape(nb, t, LANES)
    return pl.pallas_call(
        functools.partial(_attn_p_kernel, t=t, hps=hps),
        grid=(MLA_HEADS // hps, nb),
        in_specs=[
            pl.BlockSpec((hps * 2 * LANES, t), lambda h, i: (h, i)),
            pl.BlockSpec((nb, t, hps * LANES), lambda h, i: (0, 0, h), pipeline_mode=pl.Buffered(1)),
            pl.BlockSpec((nb, t, LANES), lambda h, i: (0, 0, 0), pipeline_mode=pl.Buffered(1)),
            pl.BlockSpec((nb, hps * LANES, t), lambda h, i: (0, h, 0), pipeline_mode=pl.Buffered(1)),
        ],
        out_specs=pl.BlockSpec((t, hps * LANES), lambda h, i: (i, h)),
        out_shape=jax.ShapeDtypeStruct((n, hw), BF16),
        scratch_shapes=[pltpu.VMEM((hps, 1, t), F32),
                        pltpu.VMEM((hps, LANES + SUM_ROWS, t), F32),
                        pltpu.VMEM((hps, t, t + S_PAD), F32), pltpu.VMEM((hps, t, t + S_PAD), F32),
                        pltpu.VMEM((hps, 1, t), F32), pltpu.VMEM((hps, 1, t), F32)],
        compiler_params=_params("arbitrary", "arbitrary"),
        name="attn_prompt",
    )(qt, kn3, kr3, vt3)


def _attn_s_kernel(qn_ref, qr_ref, latp_ref, krp_ref, latn_ref, krn_ref, wuk_ref, wuv_ref, o_ref,
                   *, past, ts, bps):
    rows = [slice(e * ts, (e + 1) * ts) for e in range(bps)]
    heads = [slice(h * LANES, (h + 1) * LANES) for h in range(MLA_HEADS)]
    lat = [jnp.concatenate([latp_ref[e].astype(BF16), latn_ref[r, :].astype(BF16)], axis=0)
           for e, r in enumerate(rows)]
    kr = [jnp.concatenate([krp_ref[e].astype(BF16), krn_ref[r, :].astype(BF16)], axis=0)
          for e, r in enumerate(rows)]
    q_lat_h = [_dot_nt(qn_ref[:, sl], wuk_ref[:, sl]).astype(BF16) for sl in heads]
    q_lat = [jnp.concatenate([qh[r] for qh in q_lat_h], axis=0) for r in rows]
    q_rope = [jnp.concatenate([qr_ref[r, sl.start:sl.start + MLA_ROPE] for sl in heads], axis=0)
              for r in rows]
    s = [_dot_nt(q_lat[e], lat[e]) + _dot_nt(q_rope[e], kr[e]) for e in range(bps)]
    if (past + ts - 1) // CHUNK > past // CHUNK:
        n_keys = past + ts
        qrow = lax.broadcasted_iota(jnp.int32, (MLA_HEADS * ts, n_keys), 0)
        keys = lax.broadcasted_iota(jnp.int32, (MLA_HEADS * ts, n_keys), 1)
        keep = keys // CHUNK <= (past + qrow % ts) // CHUNK
        s = [jnp.where(keep, si, NEG) for si in s]
    p = [jnp.exp2(si - jnp.max(si, axis=1, keepdims=True)) for si in s]
    l = [jnp.sum(pi, axis=1, keepdims=True) for pi in p]
    o_lat = [_dot(pi.astype(BF16), la) for pi, la in zip(p, lat)]
    o_lat = [(oi / li).astype(BF16) for oi, li in zip(o_lat, l)]
    for h, sl in enumerate(heads):
        o_h = jnp.concatenate([oi[h * ts:(h + 1) * ts] for oi in o_lat], axis=0)
        o_ref[:, sl] = _dot(o_h, wuv_ref[:, sl]).astype(BF16)


def _attn_sample(qn, qr, lat_past, kr_past, lat_new, kr_new, wuk, wuv, ts, bps):
    n, hw = qn.shape
    nb, past, _ = lat_past.shape
    new = lambda w: pl.BlockSpec((bps * ts, w), lambda b: (b, 0))
    old = lambda w: pl.BlockSpec((bps, past, w), lambda b: (b, 0, 0))
    return pl.pallas_call(
        functools.partial(_attn_s_kernel, past=past, ts=ts, bps=bps),
        grid=(nb // bps,),
        in_specs=[new(hw), new(hw), old(MLA_KV_RANK), old(MLA_ROPE), new(MLA_KV_RANK),
                  new(MLA_ROPE), _const_spec(wuk.shape), _const_spec(wuv.shape)],
        out_specs=new(hw),
        out_shape=jax.ShapeDtypeStruct((n, hw), BF16),
        compiler_params=_params("arbitrary"),
        name="attn_sample",
    )(qn, qr, lat_past, kr_past, lat_new, kr_new, wuk, wuv)


def _post_kernel(x_ref, mix_ref, xq_ref, mk_ref, mv_ref, wo_ref, g1_ref, g2_ref, wup_ref,
                 wdn_ref, g3_ref, y_ref, *, nb, tb, interleaved):
    def mem_head(ref, b, h):
        if interleaved:
            return ref[b, pl.ds(h, ref.shape[1] // X_HEADS, stride=X_HEADS), :].astype(BF16)
        return ref[b, :, h * X_DIM:(h + 1) * X_DIM].astype(BF16)

    pairs = [(b, h) for b in range(nb) for h in range(X_HEADS)]
    s = [_dot_nt(xq_ref[b * tb:(b + 1) * tb, h * X_DIM:(h + 1) * X_DIM], mem_head(mk_ref, b, h))
         for b, h in pairs]
    p = [jnp.exp2(si - jnp.max(si, axis=1, keepdims=True)) for si in s]
    l = [jnp.sum(pi, axis=1, keepdims=True) for pi in p]
    o = [_dot(pi.astype(BF16), mem_head(mv_ref, b, h)) for pi, (b, h) in zip(p, pairs)]
    o = [(oi / li).astype(BF16) for oi, li in zip(o, l)]
    cross = [jnp.concatenate(o[b * X_HEADS:(b + 1) * X_HEADS], axis=1) for b in range(nb)]
    cross = cross[0] if nb == 1 else jnp.concatenate(cross, axis=0)
    cat = jnp.concatenate([mix_ref[...], cross], axis=1)
    x1 = x_ref[...] + _rms(_dot(cat, wo_ref[...]), g1_ref[...])

    hb = _rms(x1, g2_ref[...]).astype(BF16)
    d_ff = wup_ref.shape[1]
    step = 1024
    acc = None
    for c in range(d_ff // step):
        up = jnp.maximum(_dot(hb, wup_ref[:, c * step:(c + 1) * step]), 0.0)
        dn = _dot((up * up).astype(BF16), wdn_ref[c * step:(c + 1) * step, :])
        acc = dn if acc is None else acc + dn
    y_ref[...] = x1 + _rms(acc, g3_ref[...])


def _post(x, mix, xq, mk, mv, layer, wo, g1, g2, wup, wdn, g3, tm, tb):
    n, d = x.shape
    nb = tm // tb
    xw = X_HEADS * X_DIM
    _, nmem_b, mrows, mwidth = mk.shape
    interleaved = mwidth == X_DIM
    row = lambda w: pl.BlockSpec((tm, w), lambda i: (i, 0))
    if nmem_b * tb == n:
        mem = pl.BlockSpec((None, nb, mrows, mwidth), lambda i: (layer, i, 0, 0))
    else:
        assert nmem_b == 1 and nb == 1
        mem = pl.BlockSpec((None, 1, mrows, mwidth), lambda i: (layer, 0, 0, 0))
    return pl.pallas_call(
        functools.partial(_post_kernel, nb=nb, tb=tb, interleaved=interleaved),
        grid=(n // tm,),
        in_specs=[row(d), row(mix.shape[1]), row(xw), mem, mem, _const_spec(wo.shape),
                  _const_spec(g1.shape), _const_spec(g2.shape), _layer_spec(wup.shape, layer),
                  _layer_spec(wdn.shape, layer), _const_spec(g3.shape)],
        out_specs=row(d),
        out_shape=jax.ShapeDtypeStruct((n, d), F32),
        compiler_params=_params("arbitrary"),
        name="post",
    )(x, mix, xq, mk, mv, wo, g1, g2, wup, wdn, g3)


def _pre1_kernel(x_ref, g_ref, w_ref, lb_ref,
                 qs_ref, lf_ref, kk_ref, v_ref, gs_ref, xq_ref, *, layer):
    hb = _rms(x_ref[...], g_ref[...]).astype(BF16)
    w = HG_HEADS * HG_DIM
    lb_all = lb_ref[...]
    e = jnp.exp(lb_all - jnp.max(lb_all, axis=0, keepdims=True))
    soft = e / jnp.sum(e, axis=0, keepdims=True)
    lb = jnp.sum(soft[:layer + 1], axis=0, keepdims=True) - soft[0:1]
    q = _dot(hb, w_ref[:, :w])
    qs_ref[...] = q * _sigmoid(q)
    f = _dot(hb, w_ref[:, w:2 * w])
    lf_ref[...] = jnp.log(lb + (1.0 - lb) * _sigmoid(f))
    kk_ref[...] = (1.0 - lb) * _sigmoid(-f)
    v_ref[...] = _dot(hb, w_ref[:, 2 * w:3 * w]).astype(BF16)
    g = _dot(hb, w_ref[:, 3 * w:4 * w])
    gs_ref[...] = g * _sigmoid(g)
    xq_ref[...] = (_dot(hb, w_ref[:, 4 * w:]) * (X_DIM ** -0.5 * LOG2E)).astype(BF16)


def _pre1(x, g, w, lb, layer, tm):
    n, d = x.shape
    hw = HG_HEADS * HG_DIM
    xw = X_HEADS * X_DIM
    row = lambda wd: pl.BlockSpec((tm, wd), lambda i: (i, 0))
    f32o = jax.ShapeDtypeStruct((n, hw), F32)
    return pl.pallas_call(
        functools.partial(_pre1_kernel, layer=layer),
        grid=(n // tm,),
        in_specs=[row(d), _const_spec(g.shape), _const_spec(w.shape), _const_spec(lb.shape)],
        out_specs=[row(hw), row(hw), row(hw), row(hw), row(hw), row(xw)],
        out_shape=[f32o, f32o, f32o, jax.ShapeDtypeStruct((n, hw), BF16), f32o,
                   jax.ShapeDtypeStruct((n, xw), BF16)],
        compiler_params=_params("arbitrary"),
        name="pre1",
    )(x, g, w, lb)


def _split3(x):
    hi = x.astype(BF16)
    r1 = x - hi.astype(F32)
    mid = r1.astype(BF16)
    lo = (r1 - mid.astype(F32)).astype(BF16)
    return hi, mid, lo


def _gla_block(qs_ref, lf_ref, kk_ref, v_ref, gs_ref, gn, mix_ref, row0, cb, st=None,
               s0_ref=None, sfin_ref=None):
    t = cb * CHUNK
    d = HG_DIM
    independent = st is None
    row = lax.broadcasted_iota(jnp.int32, (t, t), 0)
    col = lax.broadcasted_iota(jnp.int32, (t, t), 1)
    causal = jnp.logical_and(row // CHUNK == col // CHUNK, col <= row)
    tri = jnp.where(causal, 1.0, 0.0).astype(BF16)
    hw = HG_HEADS * d
    heads = [slice(h * d, (h + 1) * d) for h in range(HG_HEADS)]

    bb = _dot(tri, jnp.concatenate(_split3(lf_ref[...]), axis=1))
    b = (bb[:, :hw] + bb[:, hw:2 * hw]) + bb[:, 2 * hw:]
    bcast = lambda r: jnp.concatenate(
        [jnp.broadcast_to(b[c * CHUNK + r:c * CHUNK + r + 1, :], (CHUNK, hw)) for c in range(cb)],
        axis=0)
    ref = bcast(CHUNK // 2)
    last = bcast(CHUNK - 1)
    qs = qs_ref[...]
    kk = kk_ref[...]
    v = v_ref[...]
    qe = (qs * jnp.exp(b - ref)).astype(BF16)
    ke = (kk * jnp.exp(ref - b)).astype(BF16)
    kh = (kk * jnp.exp(last - b)).astype(BF16)
    qb = (qs * jnp.exp(b)).astype(BF16)
    a = [jnp.where(causal, _dot_nt(qe[:, sl], ke[:, sl]), 0.0).astype(BF16) for sl in heads]
    o = [_dot(a[h], v[:, sl]) for h, sl in enumerate(heads)]
    inter = [[] for _ in heads]
    for c in range(cb):
        rows = slice(c * CHUNK, (c + 1) * CHUNK)
        decay = jnp.exp(b[c * CHUNK + CHUNK - 1:c * CHUNK + CHUNK, :])
        if independent:
            st = [s0_ref[c, h].T for h in range(HG_HEADS)]
        for h, sl in enumerate(heads):
            inter[h].append(_dot_nt(qb[rows, sl], st[h].astype(BF16)))
        st = [st[h] * decay[:, sl] + _dot_tn(v[rows, sl], kh[rows, sl])
              for h, sl in enumerate(heads)]
        if independent:
            for h in range(HG_HEADS):
                sfin_ref[c, h] = st[h].T
    for h, sl in enumerate(heads):
        o_h = o[h] + (inter[h][0] if cb == 1 else jnp.concatenate(inter[h], axis=0))
        mix_ref[row0:row0 + t, sl] = (_rms(o_h, gn) * gs_ref[:, sl]).astype(BF16)
    return st


def _gla_kernel(qs_ref, lf_ref, kk_ref, v_ref, gs_ref, s0_ref, gn_ref, mix_ref, sfin_ref, *, cb):
    _gla_block(qs_ref, lf_ref, kk_ref, v_ref, gs_ref, gn_ref[...], mix_ref, 0, cb,
               s0_ref=s0_ref, sfin_ref=sfin_ref)


def _gla(qs, lf, kk, v, gs, s0, gn, cb):
    n, hw = qs.shape
    t = cb * CHUNK
    row = pl.BlockSpec((t, hw), lambda b: (b, 0))
    st = pl.BlockSpec((cb, HG_HEADS, HG_DIM, HG_DIM), lambda b: (b, 0, 0, 0))
    return pl.pallas_call(
        functools.partial(_gla_kernel, cb=cb),
        grid=(n // t,),
        in_specs=[row, row, row, row, row, st, _const_spec(gn.shape)],
        out_specs=[row, st],
        out_shape=[jax.ShapeDtypeStruct((n, hw), BF16),
                   jax.ShapeDtypeStruct(s0.shape, F32)],
        compiler_params=_params("arbitrary"),
        name="gla",
    )(qs, lf, kk, v, gs, s0, gn)


def _hgrn_seq_kernel(x0_ref, xa_ref, xb_ref, g_ref, w_ref, lb_ref, gn_ref, s0_ref,
                     mix_ref, xq_ref, sfin_ref, *scratch, layer, cb):
    slot_a, slot_b, st_ref = scratch[:6], scratch[6:12], scratch[12]
    i = pl.program_id(0)
    t = cb * CHUNK
    project = functools.partial(_pre1_kernel, layer=layer)

    @pl.when(i == 0)
    def _():
        project(x0_ref, g_ref, w_ref, lb_ref, *slot_a)
        for h in range(HG_HEADS):
            st_ref[h] = s0_ref[0, h].T

    gn = gn_ref[...]
    st = [st_ref[h] for h in range(HG_HEADS)]
    for half, (cur, nxt, x_ref) in enumerate(((slot_a, slot_b, xa_ref), (slot_b, slot_a, xb_ref))):
        project(x_ref, g_ref, w_ref, lb_ref, *nxt)
        st = _gla_block(*cur[:5], gn, mix_ref, half * t, cb, st=st)
        xq_ref[half * t:(half + 1) * t, :] = cur[5][...]
    for h in range(HG_HEADS):
        st_ref[h] = st[h]

    @pl.when(i == pl.num_programs(0) - 1)
    def _():
        for h in range(HG_HEADS):
            sfin_ref[0, h] = st_ref[h].T


def _hgrn_seq(x, g, w, lb, layer, s0, gn, cb):
    n, d = x.shape
    hw = HG_HEADS * HG_DIM
    xw = X_HEADS * X_DIM
    t = cb * CHUNK
    nblk = n // t
    assert nblk % 2 == 0 and s0.shape[0] == 1
    xspec = lambda f: pl.BlockSpec((t, d), lambda i: (f(i), 0))
    out = lambda wd: pl.BlockSpec((2 * t, wd), lambda i: (i, 0))
    st = pl.BlockSpec((1, HG_HEADS, HG_DIM, HG_DIM), lambda i: (0, 0, 0, 0))
    slot = [pltpu.VMEM((t, hw), F32), pltpu.VMEM((t, hw), F32), pltpu.VMEM((t, hw), F32),
            pltpu.VMEM((t, hw), BF16), pltpu.VMEM((t, hw), F32), pltpu.VMEM((t, xw), BF16)]
    return pl.pallas_call(
        functools.partial(_hgrn_seq_kernel, layer=layer, cb=cb),
        grid=(nblk // 2,),
        in_specs=[xspec(lambda i: 0), xspec(lambda i: 2 * i + 1),
                  xspec(lambda i: jnp.minimum(2 * i + 2, nblk - 1)),
                  _const_spec(g.shape), _const_spec(w.shape), _const_spec(lb.shape),
                  _const_spec(gn.shape), st],
        out_specs=[out(hw), out(xw), st],
        out_shape=[jax.ShapeDtypeStruct((n, hw), BF16), jax.ShapeDtypeStruct((n, xw), BF16),
                   jax.ShapeDtypeStruct(s0.shape, F32)],
        scratch_shapes=slot + slot + [pltpu.VMEM((HG_HEADS, HG_DIM, HG_DIM), F32)],
        compiler_params=_params("arbitrary"),
        name="hgrn_seq",
    )(x, x, x, g, w, lb, gn, s0)


def _rope_tables(tile_pos, tile_offsets):
    half = MLA_ROPE // 2
    inv = jnp.power(ROPE_THETA, -jnp.arange(half, dtype=F32) / half)

    def cos_sin(pos):
        ang = pos.astype(F32)[:, None] * inv[None, :]
        return jnp.cos(ang), jnp.sin(ang)

    pat = lambda a: jnp.concatenate([a, a, jnp.zeros_like(a), jnp.zeros_like(a)], axis=1)
    cos, sin = cos_sin(tile_pos)
    cos0, sin0 = cos_sin(tile_offsets)
    wide = lambda a: jnp.broadcast_to(a[:, :, None], a.shape + (LANES,))
    return (pat(cos), pat(sin), pat(cos0)[:, None, :], pat(sin0)[:, None, :],
            cos.T, sin.T, wide(cos0), wide(sin0))


def _prep_weights(mla_w_in, mla_w_uq, mla_w_uk, mla_w_uv):
    o1 = MLA_Q_RANK + MLA_KV_RANK
    o2 = o1 + MLA_ROPE
    d = mla_w_in.shape[0]
    win = jnp.concatenate([mla_w_in[:, :o1], mla_w_in[:, o2:], mla_w_in[:, o1:o2],
                           jnp.zeros((d, LANES - MLA_ROPE), mla_w_in.dtype)], axis=1)
    wq = mla_w_uq.reshape(MLA_Q_RANK, MLA_HEADS, MLA_NOPE + MLA_ROPE)
    nope = wq[:, :, :MLA_NOPE].reshape(MLA_Q_RANK, MLA_HEADS * MLA_NOPE)
    rope = jnp.pad(wq[:, :, MLA_NOPE:], ((0, 0), (0, 0), (0, LANES - MLA_ROPE)))
    wuq = jnp.concatenate([nope, rope.reshape(MLA_Q_RANK, MLA_HEADS * LANES)], axis=1)
    wuk = mla_w_uk.reshape(MLA_KV_RANK, MLA_HEADS * MLA_NOPE)
    wuv = mla_w_uv.reshape(MLA_KV_RANK, MLA_HEADS * MLA_V)
    wq_t = jnp.pad(wq, ((0, 0), (0, 0), (0, LANES - MLA_ROPE))).reshape(MLA_Q_RANK, -1).T
    bf = lambda a: a.astype(BF16)
    return bf(win), bf(wuq), bf(wuk), bf(wuv), bf(wq_t), bf(wuv.T)


def kernel(x_prompt, x_sample, cache_mla_latent, cache_mla_krope, cache_hgrn_state, cache_mem_k,
           cache_mem_v, mem_prompt, ln_mix_pre, ln_mix_post, ln_ffn_pre, ln_ffn_post, mem_norm,
           w_mem_kv, mla_w_in, mla_q_norm, mla_kv_norm, mla_w_uq, mla_w_uk, mla_w_uv, mla_w_out,
           hgrn_w_in, hgrn_lb, hgrn_o_norm, hgrn_w_out, w_ffn_up, w_ffn_down):
    bp, tp, d = x_prompt.shape
    bs, ts, _ = x_sample.shape
    depth = ln_mix_pre.shape[0]
    past = cache_mla_latent.shape[2]
    n_mem = mem_prompt.shape[1]
    xw = X_HEADS * X_DIM
    assert depth == 2 and bp == 1 and ts == CHUNK and past % CHUNK == 0
    assert MLA_NOPE == LANES and MLA_V == LANES and HG_DIM == LANES and X_DIM == LANES

    row2 = lambda a: a.reshape(1, -1)
    win0, wuq, wuk, wuv, wuq_t, wuv_t = _prep_weights(mla_w_in[0], mla_w_uq[0], mla_w_uk[0],
                                                      mla_w_uv[0])
    wout = (mla_w_out[0].astype(BF16), hgrn_w_out[0].astype(BF16))
    wup = w_ffn_up.astype(BF16)
    wdn = w_ffn_down.astype(BF16)
    win1 = hgrn_w_in[0].astype(BF16)

    mem_k_p, mem_v_p = _memkv(mem_prompt, mem_norm, w_mem_kv.astype(BF16))

    def trunk(x, pos, mem_k, mem_v, s0, nbatch, tm, tb, history):
        n = x.shape[0]
        tables = _rope_tables(*pos)
        pre0_args = (x, row2(ln_mix_pre[0]), win0, row2(mla_q_norm[0]), row2(mla_kv_norm[0]))
        if history is None:
            qt, kn, kr, vt3, lat, krope, xq = _pre0(
                *pre0_args, wuq_t, tables, ATTN_TILE, (wuk, wuv_t))
            mix = _attn_prompt(qt, kn, kr, vt3, ATTN_TILE, ATTN_HEADS_PER_STEP)
        else:
            qn, qr, lat, krope, xq = _pre0(*pre0_args, wuq, tables, tm)
            mix = _attn_sample(qn, qr, history[0], history[1], lat, krope, wuk, wuv, ts,
                               SAMPLE_ATTN_BATCH)
        x = _post(x, mix, xq, mem_k, mem_v, 0, wout[0],
                  row2(ln_mix_post[0]), row2(ln_ffn_pre[0]), wup, wdn,
                  row2(ln_ffn_post[0]), tm, tb)
        if n // nbatch == CHUNK:
            qs, lf, kk, vv, gs, xq = _pre1(x, row2(ln_mix_pre[1]), win1, hgrn_lb, 1, PRE1_TILE)
            mix, s_fin = _gla(qs, lf, kk, vv, gs, s0, row2(hgrn_o_norm[0]), GLA_CHUNKS)
        else:
            mix, xq, s_fin = _hgrn_seq(x, row2(ln_mix_pre[1]), win1, hgrn_lb, 1, s0,
                                       row2(hgrn_o_norm[0]), GLA_CHUNKS)
        x = _post(x, mix, xq, mem_k, mem_v, 1, wout[1],
                  row2(ln_mix_post[1]), row2(ln_ffn_pre[1]), wup, wdn,
                  row2(ln_ffn_post[1]), tm, tb)
        return x, lat, krope, s_fin

    s0_p = jnp.zeros((bp, HG_HEADS, HG_DIM, HG_DIM), F32)
    pos_p = (jnp.arange(ATTN_TILE), ATTN_TILE * jnp.arange(bp * tp // ATTN_TILE))
    y_p, lat_p, kr_p, st_p = trunk(x_prompt.reshape(bp * tp, d), pos_p, mem_k_p, mem_v_p,
                                   s0_p, bp, TOKEN_TILE, TOKEN_TILE, None)
    pos_s = (jnp.tile(past + jnp.arange(ts), TOKEN_TILE // ts),
             jnp.zeros((bs * ts // TOKEN_TILE,), jnp.int32))
    mk_s = cache_mem_k.reshape(depth, bs, n_mem * X_HEADS, X_DIM)
    mv_s = cache_mem_v.reshape(depth, bs, n_mem * X_HEADS, X_DIM)
    y_s, lat_s, kr_s, st_s = trunk(x_sample.reshape(bs * ts, d), pos_s, mk_s, mv_s,
                                   cache_hgrn_state[0], bs, TOKEN_TILE, ts,
                                   (cache_mla_latent[0], cache_mla_krope[0]))

    return (y_p.reshape(bp, tp, d), y_s.reshape(bs, ts, d),
            lat_p.reshape(1, bp, tp, MLA_KV_RANK), kr_p.reshape(1, bp, tp, MLA_ROPE),
            st_p.reshape(1, bp, HG_HEADS, HG_DIM, HG_DIM),
            mem_k_p.reshape(depth, bp, n_mem, X_HEADS, X_DIM),
            mem_v_p.reshape(depth, bp, n_mem, X_HEADS, X_DIM),
            lat_s.reshape(1, bs, ts, MLA_KV_RANK), kr_s.reshape(1, bs, ts, MLA_ROPE),
            st_s.reshape(1, bs, HG_HEADS, HG_DIM, HG_DIM))
```

```python
import functools

import jax
import jax.numpy as jnp
from jax import lax
from jax.experimental import pallas as pl
from jax.experimental.pallas import tpu as pltpu

F32 = jnp.float32
BF16 = jnp.bfloat16

CHUNK = 64
EPS = 1e-6
ROPE_THETA = 10000.0
NEG = -1e30
MLA_HEADS = 8
MLA_NOPE = 128
MLA_ROPE = 64
MLA_V = 128
MLA_Q_RANK = 384
MLA_KV_RANK = 256
HG_HEADS = 8
HG_DIM = 128
X_HEADS = 4
X_DIM = 128
LANES = 128
LOG2E = 1.4426950408889634

VMEM_LIMIT_BYTES = 52 * 1024 * 1024

TOKEN_TILE = 512
PRE1_TILE = 256
ATTN_TILE = 512
ATTN_HEADS_PER_STEP = 2
GLA_CHUNKS = 4
SAMPLE_ATTN_BATCH = 4
SUM_ROWS = 16


def _dot(a, b):
    return jnp.dot(a, b, preferred_element_type=F32)


def _dot_nt(a, b):
    return lax.dot_general(a, b, (((1,), (1,)), ((), ())), preferred_element_type=F32)


def _dot_tn(a, b):
    return lax.dot_general(a, b, (((0,), (0,)), ((), ())), preferred_element_type=F32)


def _rms(x, g):
    ms = jnp.mean(x * x, axis=-1, keepdims=True)
    return x * lax.rsqrt(ms + EPS) * g


def _sigmoid(x):
    return 1.0 / (1.0 + jnp.exp(-x))


def _params(*sem):
    return pltpu.CompilerParams(dimension_semantics=sem, vmem_limit_bytes=VMEM_LIMIT_BYTES)


def _const_spec(shape):
    nd = len(shape)
    return pl.BlockSpec(shape, lambda *_: (0,) * nd, pipeline_mode=pl.Buffered(1))


def _layer_spec(shape, layer):
    nd = len(shape) - 1
    return pl.BlockSpec((None,) + tuple(shape[1:]), lambda *_: (layer,) + (0,) * nd,
                        pipeline_mode=pl.Buffered(1))


def _memkv_kernel(mem_ref, g_ref, w_ref, k_ref, v_ref):
    h = _rms(mem_ref[0], g_ref[0]).astype(BF16)
    kv = _dot(h, w_ref[0])
    half = kv.shape[1] // 2
    k_ref[0, 0] = kv[:, :half]
    v_ref[0, 0] = kv[:, half:]


def _memkv(mem, g, w):
    bp, n_mem, d = mem.shape
    depth = g.shape[0]
    width = w.shape[2] // 2
    out = jax.ShapeDtypeStruct((depth, bp, n_mem, width), F32)
    return pl.pallas_call(
        _memkv_kernel,
        grid=(depth, bp),
        in_specs=[
            pl.BlockSpec((1, n_mem, d), lambda l, b: (b, 0, 0)),
            pl.BlockSpec((1, 1, d), lambda l, b: (l, 0, 0)),
            pl.BlockSpec((1, d, 2 * width), lambda l, b: (l, 0, 0)),
        ],
        out_specs=[
            pl.BlockSpec((1, 1, n_mem, width), lambda l, b: (l, b, 0, 0)),
            pl.BlockSpec((1, 1, n_mem, width), lambda l, b: (l, b, 0, 0)),
        ],
        out_shape=[out, out],
        compiler_params=_params("arbitrary", "arbitrary"),
        name="memkv",
    )(mem, g.reshape(depth, 1, d), w)


def _rope_rot(x, c, s1, s2):
    w = x.shape[1]
    half = MLA_ROPE // 2
    return x * c + pltpu.roll(x, half, 1) * s1 + pltpu.roll(x, w - half, 1) * s2


def _pre0_kernel(*refs, expand):
    (x_ref, g_ref, win_ref, qg_ref, kvg_ref, wq_ref,
     cos_ref, sin_ref, cos0_ref, sin0_ref) = refs[:10]
    if expand:
        (wuk_ref, wv_ref, cos_t_ref, sin_t_ref, cos0_t_ref, sin0_t_ref,
         q_ref, kn_ref, kr_ref, v_ref, lat_ref, krope_ref, xq_ref) = refs[10:]
    else:
        qn_ref, qr_ref, lat_ref, krope_ref, xq_ref = refs[10:]
    h = _rms(x_ref[...], g_ref[...]).astype(BF16)
    proj = _dot(h, win_ref[...])
    o1 = MLA_Q_RANK
    o2 = o1 + MLA_KV_RANK
    o3 = o2 + X_HEADS * X_DIM
    c_q, c_kv, xq, k_r = proj[:, :o1], proj[:, o1:o2], proj[:, o2:o3], proj[:, o3:]

    scale = (MLA_NOPE + MLA_ROPE) ** -0.5 * LOG2E
    qc = _rms(c_q, qg_ref[...]).astype(BF16)
    half = MLA_ROPE // 2
    cos0, sin0 = cos0_ref[0], sin0_ref[0]
    c = cos0 * cos_ref[...] - sin0 * sin_ref[...]
    sn = sin0 * cos_ref[...] + cos0 * sin_ref[...]
    lane = lax.broadcasted_iota(jnp.int32, sn.shape, 1)
    s1 = jnp.where(lane >= half, sn, 0.0)
    s2 = jnp.where(lane < half, -sn, 0.0)
    lat = _rms(c_kv, kvg_ref[...])
    lat_ref[...] = lat
    kr = _rope_rot(k_r, c, s1, s2)
    krope_ref[...] = kr.T[:MLA_ROPE, :]
    xq_ref[...] = (xq * (X_DIM ** -0.5 * LOG2E)).astype(BF16)
    if expand:
        latb = lat.astype(BF16)
        kn_ref[...] = _dot(latb, wuk_ref[...]).astype(BF16)
        kr_ref[...] = kr.astype(BF16)
        qt = _dot_nt(wq_ref[...], qc) * scale
        lanes4 = lambda a: jnp.concatenate([a] * (qt.shape[1] // LANES), axis=1)
        cos0_t, sin0_t = lanes4(cos0_t_ref[0]), lanes4(sin0_t_ref[0])
        cos = cos0_t * cos_t_ref[...] - sin0_t * sin_t_ref[...]
        sin = sin0_t * cos_t_ref[...] + cos0_t * sin_t_ref[...]
        slabs = []
        for hd in range(MLA_HEADS):
            r0 = hd * 2 * LANES + MLA_NOPE
            x1, x2 = qt[r0:r0 + half], qt[r0 + half:r0 + 2 * half]
            slabs += [qt[hd * 2 * LANES:r0], x1 * cos - x2 * sin, x2 * cos + x1 * sin,
                      qt[r0 + 2 * half:(hd + 1) * 2 * LANES]]
        q_ref[...] = jnp.concatenate(slabs, axis=0).astype(BF16)
        v_ref[0] = _dot_nt(wv_ref[...], latb).astype(BF16)
    else:
        q = _dot(qc, wq_ref[...]) * scale
        nope_w = MLA_HEADS * MLA_NOPE
        qn_ref[...] = q[:, :nope_w].astype(BF16)
        tile = lambda t: jnp.concatenate([t] * MLA_HEADS, axis=1)
        qr_ref[...] = _rope_rot(q[:, nope_w:], tile(c), tile(s1), tile(s2)).astype(BF16)


def _pre0(x, g, win, qg, kvg, wq, tables, tm, expand_weights=None):
    n, d = x.shape
    hw = MLA_HEADS * LANES
    xw = X_HEADS * X_DIM
    expand = expand_weights is not None
    row = lambda w: pl.BlockSpec((tm, w), lambda i: (i, 0))
    col = lambda r: pl.BlockSpec((r, tm), lambda i: (0, i))
    sds = jax.ShapeDtypeStruct
    in_specs = [row(d), _const_spec(g.shape), _const_spec(win.shape), _const_spec(qg.shape),
                _const_spec(kvg.shape), _const_spec(wq.shape),
                _const_spec((tm, LANES)), _const_spec((tm, LANES)),
                pl.BlockSpec((1, 1, LANES), lambda i: (i, 0, 0)),
                pl.BlockSpec((1, 1, LANES), lambda i: (i, 0, 0))]
    common_specs = [row(MLA_KV_RANK), col(MLA_ROPE), row(xw)]
    common_shapes = [sds((n, MLA_KV_RANK), F32),
                     sds((MLA_ROPE, n), F32),
                     sds((n, xw), BF16)]
    if expand:
        half = MLA_ROPE // 2
        in_specs += [_const_spec(w.shape) for w in expand_weights]
        in_specs += [_const_spec((half, tm)), _const_spec((half, tm)),
                     pl.BlockSpec((1, half, LANES), lambda i: (i, 0, 0)),
                     pl.BlockSpec((1, half, LANES), lambda i: (i, 0, 0))]
        out_specs = [col(2 * hw), row(hw), row(LANES), pl.BlockSpec((1, hw, tm), lambda i: (i, 0, 0))]
        out_shape = [sds((2 * hw, n), BF16),
                     sds((n, hw), BF16),
                     sds((n, LANES), BF16),
                     sds((n // tm, hw, tm), BF16)]
        operands = (x, g, win, qg, kvg, wq) + tuple(tables[:4]) + tuple(expand_weights) + tuple(tables[4:])
    else:
        out_specs = [row(hw), row(hw)]
        out_shape = [sds((n, hw), BF16),
                     sds((n, hw), BF16)]
        operands = (x, g, win, qg, kvg, wq) + tuple(tables[:4])
    return pl.pallas_call(
        functools.partial(_pre0_kernel, expand=expand),
        grid=(n // tm,),
        in_specs=in_specs,
        out_specs=out_specs + common_specs,
        out_shape=out_shape + common_shapes,
        compiler_params=_params("arbitrary"),
        name="pre0",
    )(*operands)


def _attn_p_kernel(q_ref, kn_ref, kr_ref, v_ref, o_ref, m_ref, acc_ref, sa_ref, sb_ref,
                   mxa_ref, mxb_ref, *, t, hps):
    qi = pl.program_id(1)
    m_ref[...] = jnp.full(m_ref.shape, NEG, F32)
    acc_ref[...] = jnp.zeros(acc_ref.shape, F32)
    ones = jnp.ones((SUM_ROWS, t), BF16)

    def scores(j, s_ref, mx_ref):
        kr = kr_ref[j]
        for hh in range(hps):
            k_blk = jnp.concatenate([kn_ref[j, :, hh * LANES:(hh + 1) * LANES], kr], axis=1)
            s = _dot(k_blk, q_ref[hh * 2 * LANES:(hh + 1) * 2 * LANES, :])
            s_ref[hh] = s
            mx_ref[hh] = jnp.max(s, axis=0, keepdims=True)

    def update(j, s_ref, mx_ref, masked):
        if masked:
            keep = (lax.broadcasted_iota(jnp.int32, (t, t), 0) // CHUNK
                    <= lax.broadcasted_iota(jnp.int32, (t, t), 1) // CHUNK)
        for hh in range(hps):
            s = s_ref[hh]
            if masked:
                s = jnp.where(keep, s, NEG)
                m_blk = jnp.max(s, axis=0, keepdims=True)
            else:
                m_blk = mx_ref[hh]
            m_prev = m_ref[hh]
            m_new = jnp.maximum(m_prev, m_blk)
            alpha = jnp.exp2(m_prev - m_new)
            p = jnp.exp2(s - m_new).astype(BF16)
            v_aug = jnp.concatenate([v_ref[j, hh * LANES:(hh + 1) * LANES, :], ones], axis=0)
            acc_ref[hh] = alpha * acc_ref[hh] + _dot(v_aug, p)
            m_ref[hh] = m_new

    scores(0, sa_ref, mxa_ref)

    def pair(i, _):
        scores(2 * i + 1, sb_ref, mxb_ref)
        update(2 * i, sa_ref, mxa_ref, False)
        scores(2 * i + 2, sa_ref, mxa_ref)
        update(2 * i + 1, sb_ref, mxb_ref, False)
        return 0

    lax.fori_loop(0, qi // 2, pair, 0)

    @pl.when(qi % 2 == 0)
    def _():
        update(qi, sa_ref, mxa_ref, True)

    @pl.when(qi % 2 == 1)
    def _():
        scores(qi, sb_ref, mxb_ref)
        update(qi - 1, sa_ref, mxa_ref, False)
        update(qi, sb_ref, mxb_ref, True)

    for hh in range(hps):
        o = acc_ref[hh, :LANES, :] / acc_ref[hh, LANES:LANES + 1, :]
        o_ref[:, hh * LANES:(hh + 1) * LANES] = o.T.astype(BF16)


def _attn_prompt(qt, kn, kr, vt3, t, hps):
    n, hw = kn.shape
    nb = n // t
    kn3 = kn.reshape(nb, t, hw)
    kr3 = kr.reshape(nb, t, LANES)
    return pl.pallas_call(
        functools.partial(_attn_p_kernel, t=t, hps=hps),
        grid=(MLA_HEADS // hps, nb),
        in_specs=[
            pl.BlockSpec((hps * 2 * LANES, t), lambda h, i: (h, i)),
            pl.BlockSpec((nb, t, hps * LANES), lambda h, i: (0, 0, h)),
            pl.BlockSpec((nb, t, LANES), lambda h, i: (0, 0, 0), pipeline_mode=pl.Buffered(1)),
            pl.BlockSpec((nb, hps * LANES, t), lambda h, i: (0, h, 0)),
        ],
        out_specs=pl.BlockSpec((t, hps * LANES), lambda h, i: (i, h)),
        out_shape=jax.ShapeDtypeStruct((n, hw), BF16),
        scratch_shapes=[pltpu.VMEM((hps, 1, t), F32),
                        pltpu.VMEM((hps, LANES + SUM_ROWS, t), F32),
                        pltpu.VMEM((hps, t, t), F32), pltpu.VMEM((hps, t, t), F32),
                        pltpu.VMEM((hps, 1, t), F32), pltpu.VMEM((hps, 1, t), F32)],
        compiler_params=_params("arbitrary", "arbitrary"),
        name="attn_prompt",
    )(qt, kn3, kr3, vt3)


def _attn_s_kernel(qn_ref, qr_ref, latp_ref, krp_ref, latn_ref, krn_ref, wuk_ref, wuv_ref, o_ref,
                   *, past, ts, bps):
    rows = [slice(e * ts, (e + 1) * ts) for e in range(bps)]
    heads = [slice(h * LANES, (h + 1) * LANES) for h in range(MLA_HEADS)]
    lat = [jnp.concatenate([latp_ref[e].astype(BF16), latn_ref[r, :].astype(BF16)], axis=0)
           for e, r in enumerate(rows)]
    q_lat_h = [_dot_nt(qn_ref[:, sl], wuk_ref[:, sl]).astype(BF16) for sl in heads]
    q_lat = [jnp.concatenate([qh[r] for qh in q_lat_h], axis=0) for r in rows]
    q_rope = [jnp.concatenate([qr_ref[r, sl.start:sl.start + MLA_ROPE] for sl in heads], axis=0)
              for r in rows]
    s = [_dot_nt(q_lat[e], lat[e])
         + jnp.concatenate([_dot(q_rope[e], krp_ref[e].astype(BF16)),
                            _dot(q_rope[e], krn_ref[:, r].astype(BF16))], axis=1)
         for e, r in enumerate(rows)]
    if (past + ts - 1) // CHUNK > past // CHUNK:
        n_keys = past + ts
        qrow = lax.broadcasted_iota(jnp.int32, (MLA_HEADS * ts, n_keys), 0)
        keys = lax.broadcasted_iota(jnp.int32, (MLA_HEADS * ts, n_keys), 1)
        keep = keys // CHUNK <= (past + qrow % ts) // CHUNK
        s = [jnp.where(keep, si, NEG) for si in s]
    p = [jnp.exp2(si - jnp.max(si, axis=1, keepdims=True)) for si in s]
    l = [jnp.sum(pi, axis=1, keepdims=True) for pi in p]
    o_lat = [_dot(pi.astype(BF16), la) for pi, la in zip(p, lat)]
    o_lat = [(oi / li).astype(BF16) for oi, li in zip(o_lat, l)]
    for h, sl in enumerate(heads):
        o_h = jnp.concatenate([oi[h * ts:(h + 1) * ts] for oi in o_lat], axis=0)
        o_ref[:, sl] = _dot(o_h, wuv_ref[:, sl]).astype(BF16)


def _attn_sample(qn, qr, lat_past, kr_past, lat_new, kr_new, wuk, wuv, ts, bps):
    n, hw = qn.shape
    nb, past, _ = lat_past.shape
    new = lambda w: pl.BlockSpec((bps * ts, w), lambda b: (b, 0))
    old = lambda w: pl.BlockSpec((bps, past, w), lambda b: (b, 0, 0))
    old_t = pl.BlockSpec((bps, MLA_ROPE, past), lambda b: (b, 0, 0))
    new_t = pl.BlockSpec((MLA_ROPE, bps * ts), lambda b: (0, b))
    return pl.pallas_call(
        functools.partial(_attn_s_kernel, past=past, ts=ts, bps=bps),
        grid=(nb // bps,),
        in_specs=[new(hw), new(hw), old(MLA_KV_RANK), old_t, new(MLA_KV_RANK),
                  new_t, _const_spec(wuk.shape), _const_spec(wuv.shape)],
        out_specs=new(hw),
        out_shape=jax.ShapeDtypeStruct((n, hw), BF16),
        compiler_params=_params("arbitrary"),
        name="attn_sample",
    )(qn, qr, lat_past, kr_past, lat_new, kr_new, wuk, wuv)


def _post_kernel(x_ref, mix_ref, xq_ref, mk_ref, mv_ref, wo_ref, g1_ref, g2_ref, wup_ref,
                 wdn_ref, g3_ref, y_ref, *, nb, tb, interleaved):
    def mem_head(ref, b, h):
        if interleaved:
            return ref[b, pl.ds(h, ref.shape[1] // X_HEADS, stride=X_HEADS), :].astype(BF16)
        return ref[b, :, h * X_DIM:(h + 1) * X_DIM].astype(BF16)

    pairs = [(b, h) for b in range(nb) for h in range(X_HEADS)]
    s = [_dot_nt(xq_ref[b * tb:(b + 1) * tb, h * X_DIM:(h + 1) * X_DIM], mem_head(mk_ref, b, h))
         for b, h in pairs]
    p = [jnp.exp2(si - jnp.max(si, axis=1, keepdims=True)) for si in s]
    l = [jnp.sum(pi, axis=1, keepdims=True) for pi in p]
    o = [_dot(pi.astype(BF16), mem_head(mv_ref, b, h)) for pi, (b, h) in zip(p, pairs)]
    o = [(oi / li).astype(BF16) for oi, li in zip(o, l)]
    cross = [jnp.concatenate(o[b * X_HEADS:(b + 1) * X_HEADS], axis=1) for b in range(nb)]
    cross = cross[0] if nb == 1 else jnp.concatenate(cross, axis=0)
    cat = jnp.concatenate([mix_ref[...], cross], axis=1)
    x1 = x_ref[...] + _rms(_dot(cat, wo_ref[...]), g1_ref[...])

    hb = _rms(x1, g2_ref[...]).astype(BF16)
    d_ff = wup_ref.shape[1]
    step = 1024
    acc = None
    for c in range(d_ff // step):
        up = jnp.maximum(_dot(hb, wup_ref[:, c * step:(c + 1) * step]), 0.0)
        dn = _dot((up * up).astype(BF16), wdn_ref[c * step:(c + 1) * step, :])
        acc = dn if acc is None else acc + dn
    y_ref[...] = x1 + _rms(acc, g3_ref[...])


def _post(x, mix, xq, mk, mv, layer, wo, g1, g2, wup, wdn, g3, tm, tb):
    n, d = x.shape
    nb = tm // tb
    xw = X_HEADS * X_DIM
    _, nmem_b, mrows, mwidth = mk.shape
    interleaved = mwidth == X_DIM
    row = lambda w: pl.BlockSpec((tm, w), lambda i: (i, 0))
    if nmem_b * tb == n:
        mem = pl.BlockSpec((None, nb, mrows, mwidth), lambda i: (layer, i, 0, 0))
    else:
        assert nmem_b == 1 and nb == 1
        mem = pl.BlockSpec((None, 1, mrows, mwidth), lambda i: (layer, 0, 0, 0))
    return pl.pallas_call(
        functools.partial(_post_kernel, nb=nb, tb=tb, interleaved=interleaved),
        grid=(n // tm,),
        in_specs=[row(d), row(mix.shape[1]), row(xw), mem, mem, _const_spec(wo.shape),
                  _const_spec(g1.shape), _const_spec(g2.shape), _layer_spec(wup.shape, layer),
                  _layer_spec(wdn.shape, layer), _const_spec(g3.shape)],
        out_specs=row(d),
        out_shape=jax.ShapeDtypeStruct((n, d), F32),
        compiler_params=_params("arbitrary"),
        name="post",
    )(x, mix, xq, mk, mv, wo, g1, g2, wup, wdn, g3)


def _pre1_kernel(x_ref, g_ref, w_ref, lb_ref,
                 qs_ref, lf_ref, kk_ref, v_ref, gs_ref, xq_ref, *, layer):
    hb = _rms(x_ref[...], g_ref[...]).astype(BF16)
    w = HG_HEADS * HG_DIM
    lb_all = lb_ref[...]
    e = jnp.exp(lb_all - jnp.max(lb_all, axis=0, keepdims=True))
    soft = e / jnp.sum(e, axis=0, keepdims=True)
    lb = jnp.sum(soft[:layer + 1], axis=0, keepdims=True) - soft[0:1]
    q = _dot(hb, w_ref[:, :w])
    qs_ref[...] = q * _sigmoid(q)
    f = _dot(hb, w_ref[:, w:2 * w])
    lf_ref[...] = jnp.log(lb + (1.0 - lb) * _sigmoid(f))
    kk_ref[...] = (1.0 - lb) * _sigmoid(-f)
    v_ref[...] = _dot(hb, w_ref[:, 2 * w:3 * w]).astype(BF16)
    g = _dot(hb, w_ref[:, 3 * w:4 * w])
    gs_ref[...] = g * _sigmoid(g)
    xq_ref[...] = (_dot(hb, w_ref[:, 4 * w:]) * (X_DIM ** -0.5 * LOG2E)).astype(BF16)


def _pre1(x, g, w, lb, layer, tm):
    n, d = x.shape
    hw = HG_HEADS * HG_DIM
    xw = X_HEADS * X_DIM
    row = lambda wd: pl.BlockSpec((tm, wd), lambda i: (i, 0))
    f32o = jax.ShapeDtypeStruct((n, hw), F32)
    return pl.pallas_call(
        functools.partial(_pre1_kernel, layer=layer),
        grid=(n // tm,),
        in_specs=[row(d), _const_spec(g.shape), _const_spec(w.shape), _const_spec(lb.shape)],
        out_specs=[row(hw), row(hw), row(hw), row(hw), row(hw), row(xw)],
        out_shape=[f32o, f32o, f32o, jax.ShapeDtypeStruct((n, hw), BF16), f32o,
                   jax.ShapeDtypeStruct((n, xw), BF16)],
        compiler_params=_params("arbitrary"),
        name="pre1",
    )(x, g, w, lb)


def _split3(x):
    hi = x.astype(BF16)
    r1 = x - hi.astype(F32)
    mid = r1.astype(BF16)
    lo = (r1 - mid.astype(F32)).astype(BF16)
    return hi, mid, lo


def _gla_block(qs_ref, lf_ref, kk_ref, v_ref, gs_ref, gn, mix_ref, row0, cb, st=None,
               s0_ref=None, sfin_ref=None):
    t = cb * CHUNK
    d = HG_DIM
    independent = st is None
    row = lax.broadcasted_iota(jnp.int32, (t, t), 0)
    col = lax.broadcasted_iota(jnp.int32, (t, t), 1)
    causal = jnp.logical_and(row // CHUNK == col // CHUNK, col <= row)
    tri = jnp.where(causal, 1.0, 0.0).astype(BF16)
    hw = HG_HEADS * d
    heads = [slice(h * d, (h + 1) * d) for h in range(HG_HEADS)]

    bb = _dot(tri, jnp.concatenate(_split3(lf_ref[...]), axis=1))
    b = (bb[:, :hw] + bb[:, hw:2 * hw]) + bb[:, 2 * hw:]
    bcast = lambda r: jnp.concatenate(
        [jnp.broadcast_to(b[c * CHUNK + r:c * CHUNK + r + 1, :], (CHUNK, hw)) for c in range(cb)],
        axis=0)
    ref = bcast(CHUNK // 2)
    last = bcast(CHUNK - 1)
    qs = qs_ref[...]
    kk = kk_ref[...]
    v = v_ref[...]
    qe = (qs * jnp.exp(b - ref)).astype(BF16)
    ke = (kk * jnp.exp(ref - b)).astype(BF16)
    kh = (kk * jnp.exp(last - b)).astype(BF16)
    qb = (qs * jnp.exp(b)).astype(BF16)
    a = [jnp.where(causal, _dot_nt(qe[:, sl], ke[:, sl]), 0.0).astype(BF16) for sl in heads]
    o = [_dot(a[h], v[:, sl]) for h, sl in enumerate(heads)]
    inter = [[] for _ in heads]
    for c in range(cb):
        rows = slice(c * CHUNK, (c + 1) * CHUNK)
        decay = jnp.exp(b[c * CHUNK + CHUNK - 1:c * CHUNK + CHUNK, :])
        if independent:
            st = [s0_ref[c, h].T for h in range(HG_HEADS)]
        for h, sl in enumerate(heads):
            inter[h].append(_dot_nt(qb[rows, sl], st[h].astype(BF16)))
        st = [st[h] * decay[:, sl] + _dot_tn(v[rows, sl], kh[rows, sl])
              for h, sl in enumerate(heads)]
        if independent:
            for h in range(HG_HEADS):
                sfin_ref[c, h] = st[h].T
    for h, sl in enumerate(heads):
        o_h = o[h] + (inter[h][0] if cb == 1 else jnp.concatenate(inter[h], axis=0))
        mix_ref[row0:row0 + t, sl] = (_rms(o_h, gn) * gs_ref[:, sl]).astype(BF16)
    return st


def _gla_kernel(qs_ref, lf_ref, kk_ref, v_ref, gs_ref, s0_ref, gn_ref, mix_ref, sfin_ref, *, cb):
    _gla_block(qs_ref, lf_ref, kk_ref, v_ref, gs_ref, gn_ref[...], mix_ref, 0, cb,
               s0_ref=s0_ref, sfin_ref=sfin_ref)


def _gla(qs, lf, kk, v, gs, s0, gn, cb):
    n, hw = qs.shape
    t = cb * CHUNK
    row = pl.BlockSpec((t, hw), lambda b: (b, 0))
    st = pl.BlockSpec((cb, HG_HEADS, HG_DIM, HG_DIM), lambda b: (b, 0, 0, 0))
    return pl.pallas_call(
        functools.partial(_gla_kernel, cb=cb),
        grid=(n // t,),
        in_specs=[row, row, row, row, row, st, _const_spec(gn.shape)],
        out_specs=[row, st],
        out_shape=[jax.ShapeDtypeStruct((n, hw), BF16),
                   jax.ShapeDtypeStruct(s0.shape, F32)],
        compiler_params=_params("arbitrary"),
        name="gla",
    )(qs, lf, kk, v, gs, s0, gn)


def _hgrn_seq_kernel(x0_ref, xa_ref, xb_ref, g_ref, w_ref, lb_ref, gn_ref, s0_ref,
                     mix_ref, xq_ref, sfin_ref, *scratch, layer, cb):
    slot_a, slot_b, st_ref = scratch[:6], scratch[6:12], scratch[12]
    i = pl.program_id(0)
    t = cb * CHUNK
    project = functools.partial(_pre1_kernel, layer=layer)

    @pl.when(i == 0)
    def _():
        project(x0_ref, g_ref, w_ref, lb_ref, *slot_a)
        for h in range(HG_HEADS):
            st_ref[h] = s0_ref[0, h].T

    gn = gn_ref[...]
    st = [st_ref[h] for h in range(HG_HEADS)]
    for half, (cur, nxt, x_ref) in enumerate(((slot_a, slot_b, xa_ref), (slot_b, slot_a, xb_ref))):
        project(x_ref, g_ref, w_ref, lb_ref, *nxt)
        st = _gla_block(*cur[:5], gn, mix_ref, half * t, cb, st=st)
        xq_ref[half * t:(half + 1) * t, :] = cur[5][...]
    for h in range(HG_HEADS):
        st_ref[h] = st[h]

    @pl.when(i == pl.num_programs(0) - 1)
    def _():
        for h in range(HG_HEADS):
            sfin_ref[0, h] = st_ref[h].T


def _hgrn_seq(x, g, w, lb, layer, s0, gn, cb):
    n, d = x.shape
    hw = HG_HEADS * HG_DIM
    xw = X_HEADS * X_DIM
    t = cb * CHUNK
    nblk = n // t
    assert nblk % 2 == 0 and s0.shape[0] == 1
    xspec = lambda f: pl.BlockSpec((t, d), lambda i: (f(i), 0))
    out = lambda wd: pl.BlockSpec((2 * t, wd), lambda i: (i, 0))
    st = pl.BlockSpec((1, HG_HEADS, HG_DIM, HG_DIM), lambda i: (0, 0, 0, 0))
    slot = [pltpu.VMEM((t, hw), F32), pltpu.VMEM((t, hw), F32), pltpu.VMEM((t, hw), F32),
            pltpu.VMEM((t, hw), BF16), pltpu.VMEM((t, hw), F32), pltpu.VMEM((t, xw), BF16)]
    return pl.pallas_call(
        functools.partial(_hgrn_seq_kernel, layer=layer, cb=cb),
        grid=(nblk // 2,),
        in_specs=[xspec(lambda i: 0), xspec(lambda i: 2 * i + 1),
                  xspec(lambda i: jnp.minimum(2 * i + 2, nblk - 1)),
                  _const_spec(g.shape), _const_spec(w.shape), _const_spec(lb.shape),
                  _const_spec(gn.shape), st],
        out_specs=[out(hw), out(xw), st],
        out_shape=[jax.ShapeDtypeStruct((n, hw), BF16), jax.ShapeDtypeStruct((n, xw), BF16),
                   jax.ShapeDtypeStruct(s0.shape, F32)],
        scratch_shapes=slot + slot + [pltpu.VMEM((HG_HEADS, HG_DIM, HG_DIM), F32)],
        compiler_params=_params("arbitrary"),
        name="hgrn_seq",
    )(x, x, x, g, w, lb, gn, s0)


def _rope_tables(tile_pos, tile_offsets):
    half = MLA_ROPE // 2
    inv = jnp.power(ROPE_THETA, -jnp.arange(half, dtype=F32) / half)

    def cos_sin(pos):
        ang = pos.astype(F32)[:, None] * inv[None, :]
        return jnp.cos(ang), jnp.sin(ang)

    pat = lambda a: jnp.concatenate([a, a, jnp.zeros_like(a), jnp.zeros_like(a)], axis=1)
    cos, sin = cos_sin(tile_pos)
    cos0, sin0 = cos_sin(tile_offsets)
    wide = lambda a: jnp.broadcast_to(a[:, :, None], a.shape + (LANES,))
    return (pat(cos), pat(sin), pat(cos0)[:, None, :], pat(sin0)[:, None, :],
            cos.T, sin.T, wide(cos0), wide(sin0))


def _prep_weights(mla_w_in, mla_w_uq, mla_w_uk, mla_w_uv):
    o1 = MLA_Q_RANK + MLA_KV_RANK
    o2 = o1 + MLA_ROPE
    d = mla_w_in.shape[0]
    win = jnp.concatenate([mla_w_in[:, :o1], mla_w_in[:, o2:], mla_w_in[:, o1:o2],
                           jnp.zeros((d, LANES - MLA_ROPE), mla_w_in.dtype)], axis=1)
    wq = mla_w_uq.reshape(MLA_Q_RANK, MLA_HEADS, MLA_NOPE + MLA_ROPE)
    nope = wq[:, :, :MLA_NOPE].reshape(MLA_Q_RANK, MLA_HEADS * MLA_NOPE)
    rope = jnp.pad(wq[:, :, MLA_NOPE:], ((0, 0), (0, 0), (0, LANES - MLA_ROPE)))
    wuq = jnp.concatenate([nope, rope.reshape(MLA_Q_RANK, MLA_HEADS * LANES)], axis=1)
    wuk = mla_w_uk.reshape(MLA_KV_RANK, MLA_HEADS * MLA_NOPE)
    wuv = mla_w_uv.reshape(MLA_KV_RANK, MLA_HEADS * MLA_V)
    wq_t = jnp.pad(wq, ((0, 0), (0, 0), (0, LANES - MLA_ROPE))).reshape(MLA_Q_RANK, -1).T
    bf = lambda a: a.astype(BF16)
    return bf(win), bf(wuq), bf(wuk), bf(wuv), bf(wq_t), bf(wuv.T)


def kernel(x_prompt, x_sample, cache_mla_latent, cache_mla_krope, cache_hgrn_state, cache_mem_k,
           cache_mem_v, mem_prompt, ln_mix_pre, ln_mix_post, ln_ffn_pre, ln_ffn_post, mem_norm,
           w_mem_kv, mla_w_in, mla_q_norm, mla_kv_norm, mla_w_uq, mla_w_uk, mla_w_uv, mla_w_out,
           hgrn_w_in, hgrn_lb, hgrn_o_norm, hgrn_w_out, w_ffn_up, w_ffn_down):
    bp, tp, d = x_prompt.shape
    bs, ts, _ = x_sample.shape
    depth = ln_mix_pre.shape[0]
    past = cache_mla_latent.shape[2]
    n_mem = mem_prompt.shape[1]
    xw = X_HEADS * X_DIM
    assert depth == 2 and bp == 1 and ts == CHUNK and past % CHUNK == 0
    assert MLA_NOPE == LANES and MLA_V == LANES and HG_DIM == LANES and X_DIM == LANES

    row2 = lambda a: a.reshape(1, -1)
    win0, wuq, wuk, wuv, wuq_t, wuv_t = _prep_weights(mla_w_in[0], mla_w_uq[0], mla_w_uk[0],
                                                      mla_w_uv[0])
    wout = (mla_w_out[0].astype(BF16), hgrn_w_out[0].astype(BF16))
    wup = w_ffn_up.astype(BF16)
    wdn = w_ffn_down.astype(BF16)
    win1 = hgrn_w_in[0].astype(BF16)

    mem_k_p, mem_v_p = _memkv(mem_prompt, mem_norm, w_mem_kv.astype(BF16))

    def trunk(x, pos, mem_k, mem_v, s0, nbatch, tm, tb, history):
        n = x.shape[0]
        tables = _rope_tables(*pos)
        pre0_args = (x, row2(ln_mix_pre[0]), win0, row2(mla_q_norm[0]), row2(mla_kv_norm[0]))
        if history is None:
            qt, kn, kr, vt3, lat, krope, xq = _pre0(
                *pre0_args, wuq_t, tables, ATTN_TILE, (wuk, wuv_t))
            mix = _attn_prompt(qt, kn, kr, vt3, ATTN_TILE, ATTN_HEADS_PER_STEP)
        else:
            qn, qr, lat, krope, xq = _pre0(*pre0_args, wuq, tables, tm)
            mix = _attn_sample(qn, qr, history[0], history[1], lat, krope, wuk, wuv, ts,
                               SAMPLE_ATTN_BATCH)
        x = _post(x, mix, xq, mem_k, mem_v, 0, wout[0],
                  row2(ln_mix_post[0]), row2(ln_ffn_pre[0]), wup, wdn,
                  row2(ln_ffn_post[0]), tm, tb)
        if n // nbatch == CHUNK:
            qs, lf, kk, vv, gs, xq = _pre1(x, row2(ln_mix_pre[1]), win1, hgrn_lb, 1, PRE1_TILE)
            mix, s_fin = _gla(qs, lf, kk, vv, gs, s0, row2(hgrn_o_norm[0]), GLA_CHUNKS)
        else:
            mix, xq, s_fin = _hgrn_seq(x, row2(ln_mix_pre[1]), win1, hgrn_lb, 1, s0,
                                       row2(hgrn_o_norm[0]), GLA_CHUNKS)
        x = _post(x, mix, xq, mem_k, mem_v, 1, wout[1],
                  row2(ln_mix_post[1]), row2(ln_ffn_pre[1]), wup, wdn,
                  row2(ln_ffn_post[1]), tm, tb)
        return x, lat, krope, s_fin

    s0_p = jnp.zeros((bp, HG_HEADS, HG_DIM, HG_DIM), F32)
    pos_p = (jnp.arange(ATTN_TILE), ATTN_TILE * jnp.arange(bp * tp // ATTN_TILE))
    y_p, lat_p, kr_p, st_p = trunk(x_prompt.reshape(bp * tp, d), pos_p, mem_k_p, mem_v_p,
                                   s0_p, bp, TOKEN_TILE, TOKEN_TILE, None)
    pos_s = (jnp.tile(past + jnp.arange(ts), TOKEN_TILE // ts),
             jnp.zeros((bs * ts // TOKEN_TILE,), jnp.int32))
    mk_s = cache_mem_k.reshape(depth, bs, n_mem * X_HEADS, X_DIM)
    mv_s = cache_mem_v.reshape(depth, bs, n_mem * X_HEADS, X_DIM)
    y_s, lat_s, kr_s, st_s = trunk(x_sample.reshape(bs * ts, d), pos_s, mk_s, mv_s,
                                   cache_hgrn_state[0], bs, TOKEN_TILE, ts,
                                   (cache_mla_latent[0], jnp.swapaxes(cache_mla_krope[0], 1, 2)))

    return (y_p.reshape(bp, tp, d), y_s.reshape(bs, ts, d),
            lat_p.reshape(1, bp, tp, MLA_KV_RANK), kr_p.T.reshape(1, bp, tp, MLA_ROPE),
            st_p.reshape(1, bp, HG_HEADS, HG_DIM, HG_DIM),
            mem_k_p.reshape(depth, bp, n_mem, X_HEADS, X_DIM),
            mem_v_p.reshape(depth, bp, n_mem, X_HEADS, X_DIM),
            lat_s.reshape(1, bs, ts, MLA_KV_RANK), kr_s.T.reshape(1, bs, ts, MLA_ROPE),
            st_s.reshape(1, bs, HG_HEADS, HG_DIM, HG_DIM))
```

```python
import functools

import jax
import jax.numpy as jnp
from jax import lax
from jax.experimental import pallas as pl
from jax.experimental.pallas import tpu as pltpu

F32 = jnp.float32
BF16 = jnp.bfloat16

CHUNK = 64
EPS = 1e-6
ROPE_THETA = 10000.0
NEG = -1e30
MLA_HEADS = 8
MLA_NOPE = 128
MLA_ROPE = 64
MLA_V = 128
MLA_Q_RANK = 384
MLA_KV_RANK = 256
HG_HEADS = 8
HG_DIM = 128
X_HEADS = 4
X_DIM = 128
LANES = 128
LOG2E = 1.4426950408889634

VMEM_LIMIT_BYTES = 52 * 1024 * 1024

TOKEN_TILE = 512
PRE1_TILE = 256
ATTN_TILE = 512
ATTN_HEADS_PER_STEP = 2
GLA_CHUNKS = 4
SAMPLE_ATTN_BATCH = 4
LOOP_PAIRS = 4
SUM_ROWS = 16


def _dot(a, b):
    return jnp.dot(a, b, preferred_element_type=F32)


def _dot_nt(a, b):
    return lax.dot_general(a, b, (((1,), (1,)), ((), ())), preferred_element_type=F32)


def _dot_tn(a, b):
    return lax.dot_general(a, b, (((0,), (0,)), ((), ())), preferred_element_type=F32)


def _rms(x, g):
    ms = jnp.mean(x * x, axis=-1, keepdims=True)
    return x * lax.rsqrt(ms + EPS) * g


def _sigmoid(x):
    return 1.0 / (1.0 + jnp.exp(-x))


def _params(*sem):
    return pltpu.CompilerParams(dimension_semantics=sem, vmem_limit_bytes=VMEM_LIMIT_BYTES)


def _const_spec(shape):
    nd = len(shape)
    return pl.BlockSpec(shape, lambda *_: (0,) * nd, pipeline_mode=pl.Buffered(1))


def _layer_spec(shape, layer):
    nd = len(shape) - 1
    return pl.BlockSpec((None,) + tuple(shape[1:]), lambda *_: (layer,) + (0,) * nd,
                        pipeline_mode=pl.Buffered(1))


def _memkv_kernel(mem_ref, g_ref, w_ref, k_ref, v_ref):
    h = _rms(mem_ref[0], g_ref[0]).astype(BF16)
    kv = _dot(h, w_ref[0])
    half = kv.shape[1] // 2
    k_ref[0, 0] = kv[:, :half]
    v_ref[0, 0] = kv[:, half:]


def _memkv(mem, g, w):
    bp, n_mem, d = mem.shape
    depth = g.shape[0]
    width = w.shape[2] // 2
    out = jax.ShapeDtypeStruct((depth, bp, n_mem, width), F32)
    return pl.pallas_call(
        _memkv_kernel,
        grid=(depth, bp),
        in_specs=[
            pl.BlockSpec((1, n_mem, d), lambda l, b: (b, 0, 0)),
            pl.BlockSpec((1, 1, d), lambda l, b: (l, 0, 0)),
            pl.BlockSpec((1, d, 2 * width), lambda l, b: (l, 0, 0)),
        ],
        out_specs=[
            pl.BlockSpec((1, 1, n_mem, width), lambda l, b: (l, b, 0, 0)),
            pl.BlockSpec((1, 1, n_mem, width), lambda l, b: (l, b, 0, 0)),
        ],
        out_shape=[out, out],
        compiler_params=_params("arbitrary", "arbitrary"),
        name="memkv",
    )(mem, g.reshape(depth, 1, d), w)


def _rope_rot(x, c, s1, s2):
    w = x.shape[1]
    half = MLA_ROPE // 2
    return x * c + pltpu.roll(x, half, 1) * s1 + pltpu.roll(x, w - half, 1) * s2


def _pre0_kernel(*refs, expand):
    (x_ref, g_ref, win_ref, qg_ref, kvg_ref, wq_ref,
     cos_ref, sin_ref, cos0_ref, sin0_ref) = refs[:10]
    if expand:
        (wuk_ref, wv_ref, cos_t_ref, sin_t_ref, cos0_t_ref, sin0_t_ref,
         q_ref, kn_ref, kr_ref, v_ref, lat_ref, krope_ref, xq_ref) = refs[10:]
    else:
        qn_ref, qr_ref, lat_ref, krope_ref, xq_ref = refs[10:]
    h = _rms(x_ref[...], g_ref[...]).astype(BF16)
    proj = _dot(h, win_ref[...])
    o1 = MLA_Q_RANK
    o2 = o1 + MLA_KV_RANK
    o3 = o2 + X_HEADS * X_DIM
    c_q, c_kv, xq, k_r = proj[:, :o1], proj[:, o1:o2], proj[:, o2:o3], proj[:, o3:]

    scale = (MLA_NOPE + MLA_ROPE) ** -0.5 * LOG2E
    qc = _rms(c_q, qg_ref[...]).astype(BF16)
    half = MLA_ROPE // 2
    cos0, sin0 = cos0_ref[0], sin0_ref[0]
    c = cos0 * cos_ref[...] - sin0 * sin_ref[...]
    sn = sin0 * cos_ref[...] + cos0 * sin_ref[...]
    lane = lax.broadcasted_iota(jnp.int32, sn.shape, 1)
    s1 = jnp.where(lane >= half, sn, 0.0)
    s2 = jnp.where(lane < half, -sn, 0.0)
    lat = _rms(c_kv, kvg_ref[...])
    lat_ref[...] = lat
    kr = _rope_rot(k_r, c, s1, s2)
    krope_ref[...] = kr.T[:MLA_ROPE, :]
    xq_ref[...] = (xq * (X_DIM ** -0.5 * LOG2E)).astype(BF16)
    if expand:
        latb = lat.astype(BF16)
        kn_ref[...] = _dot(latb, wuk_ref[...]).astype(BF16)
        kr_ref[...] = kr.astype(BF16)
        qt = _dot_nt(wq_ref[...], qc) * scale
        lanes4 = lambda a: jnp.concatenate([a] * (qt.shape[1] // LANES), axis=1)
        cos0_t, sin0_t = lanes4(cos0_t_ref[0]), lanes4(sin0_t_ref[0])
        cos = cos0_t * cos_t_ref[...] - sin0_t * sin_t_ref[...]
        sin = sin0_t * cos_t_ref[...] + cos0_t * sin_t_ref[...]
        slabs = []
        for hd in range(MLA_HEADS):
            r0 = hd * 2 * LANES + MLA_NOPE
            x1, x2 = qt[r0:r0 + half], qt[r0 + half:r0 + 2 * half]
            slabs += [qt[hd * 2 * LANES:r0], x1 * cos - x2 * sin, x2 * cos + x1 * sin,
                      qt[r0 + 2 * half:(hd + 1) * 2 * LANES]]
        q_ref[...] = jnp.concatenate(slabs, axis=0).astype(BF16)
        v_ref[0] = _dot_nt(wv_ref[...], latb).astype(BF16)
    else:
        q = _dot(qc, wq_ref[...]) * scale
        nope_w = MLA_HEADS * MLA_NOPE
        qn_ref[...] = q[:, :nope_w].astype(BF16)
        tile = lambda t: jnp.concatenate([t] * MLA_HEADS, axis=1)
        qr_ref[...] = _rope_rot(q[:, nope_w:], tile(c), tile(s1), tile(s2)).astype(BF16)


def _pre0(x, g, win, qg, kvg, wq, tables, tm, expand_weights=None):
    n, d = x.shape
    hw = MLA_HEADS * LANES
    xw = X_HEADS * X_DIM
    expand = expand_weights is not None
    row = lambda w: pl.BlockSpec((tm, w), lambda i: (i, 0))
    col = lambda r: pl.BlockSpec((r, tm), lambda i: (0, i))
    sds = jax.ShapeDtypeStruct
    in_specs = [row(d), _const_spec(g.shape), _const_spec(win.shape), _const_spec(qg.shape),
                _const_spec(kvg.shape), _const_spec(wq.shape),
                _const_spec((tm, LANES)), _const_spec((tm, LANES)),
                pl.BlockSpec((1, 1, LANES), lambda i: (i, 0, 0)),
                pl.BlockSpec((1, 1, LANES), lambda i: (i, 0, 0))]
    common_specs = [row(MLA_KV_RANK), col(MLA_ROPE), row(xw)]
    common_shapes = [sds((n, MLA_KV_RANK), F32),
                     sds((MLA_ROPE, n), F32),
                     sds((n, xw), BF16)]
    if expand:
        half = MLA_ROPE // 2
        in_specs += [_const_spec(w.shape) for w in expand_weights]
        in_specs += [_const_spec((half, tm)), _const_spec((half, tm)),
                     pl.BlockSpec((1, half, LANES), lambda i: (i, 0, 0)),
                     pl.BlockSpec((1, half, LANES), lambda i: (i, 0, 0))]
        out_specs = [col(2 * hw), row(hw), row(LANES), pl.BlockSpec((1, hw, tm), lambda i: (i, 0, 0))]
        out_shape = [sds((2 * hw, n), BF16),
                     sds((n, hw), BF16),
                     sds((n, LANES), BF16),
                     sds((n // tm, hw, tm), BF16)]
        operands = (x, g, win, qg, kvg, wq) + tuple(tables[:4]) + tuple(expand_weights) + tuple(tables[4:])
    else:
        out_specs = [row(hw), row(hw)]
        out_shape = [sds((n, hw), BF16),
                     sds((n, hw), BF16)]
        operands = (x, g, win, qg, kvg, wq) + tuple(tables[:4])
    return pl.pallas_call(
        functools.partial(_pre0_kernel, expand=expand),
        grid=(n // tm,),
        in_specs=in_specs,
        out_specs=out_specs + common_specs,
        out_shape=out_shape + common_shapes,
        compiler_params=_params("arbitrary"),
        name="pre0",
    )(*operands)


def _attn_p_kernel(q_ref, kn_ref, kr_ref, v_ref, o_ref, m_ref, acc_ref, sa_ref, sb_ref,
                   mxa_ref, mxb_ref, *, t, hps):
    qi = pl.program_id(1)
    m_ref[...] = jnp.full(m_ref.shape, NEG, F32)
    acc_ref[...] = jnp.zeros(acc_ref.shape, F32)
    ones = jnp.ones((SUM_ROWS, t), BF16)

    def scores(j, s_ref, mx_ref):
        kr = kr_ref[j]
        for hh in range(hps):
            k_blk = jnp.concatenate([kn_ref[j, :, hh * LANES:(hh + 1) * LANES], kr], axis=1)
            s = _dot(k_blk, q_ref[hh * 2 * LANES:(hh + 1) * 2 * LANES, :])
            s_ref[hh] = s
            mx_ref[hh] = jnp.max(s, axis=0, keepdims=True)

    def update(j, s_ref, mx_ref, masked):
        if masked:
            keep = (lax.broadcasted_iota(jnp.int32, (t, t), 0) // CHUNK
                    <= lax.broadcasted_iota(jnp.int32, (t, t), 1) // CHUNK)
        for hh in range(hps):
            s = s_ref[hh]
            if masked:
                s = jnp.where(keep, s, NEG)
                m_blk = jnp.max(s, axis=0, keepdims=True)
            else:
                m_blk = mx_ref[hh]
            m_prev = m_ref[hh]
            m_new = jnp.maximum(m_prev, m_blk)
            alpha = jnp.exp2(m_prev - m_new)
            p = jnp.exp2(s - m_new).astype(BF16)
            v_aug = jnp.concatenate([v_ref[j, hh * LANES:(hh + 1) * LANES, :], ones], axis=0)
            acc_ref[hh] = alpha * acc_ref[hh] + _dot(v_aug, p)
            m_ref[hh] = m_new

    scores(0, sa_ref, mxa_ref)

    def pair(j):
        scores(j + 1, sb_ref, mxb_ref)
        update(j, sa_ref, mxa_ref, False)
        scores(j + 2, sa_ref, mxa_ref)
        update(j + 1, sb_ref, mxb_ref, False)

    def pairs(i, _):
        for u in range(LOOP_PAIRS):
            pair(2 * (LOOP_PAIRS * i + u))
        return 0

    lax.fori_loop(0, qi // (2 * LOOP_PAIRS), pairs, 0)
    done = (qi // (2 * LOOP_PAIRS)) * LOOP_PAIRS
    for u in range(LOOP_PAIRS - 1):
        @pl.when(qi // 2 - done > u)
        def _():
            pair(2 * (done + u))

    @pl.when(qi % 2 == 0)
    def _():
        update(qi, sa_ref, mxa_ref, True)

    @pl.when(qi % 2 == 1)
    def _():
        scores(qi, sb_ref, mxb_ref)
        update(qi - 1, sa_ref, mxa_ref, False)
        update(qi, sb_ref, mxb_ref, True)

    for hh in range(hps):
        o = acc_ref[hh, :LANES, :] / acc_ref[hh, LANES:LANES + 1, :]
        o_ref[:, hh * LANES:(hh + 1) * LANES] = o.T.astype(BF16)


def _attn_prompt(qt, kn, kr, vt3, t, hps):
    n, hw = kn.shape
    nb = n // t
    kn3 = kn.reshape(nb, t, hw)
    kr3 = kr.reshape(nb, t, LANES)
    return pl.pallas_call(
        functools.partial(_attn_p_kernel, t=t, hps=hps),
        grid=(MLA_HEADS // hps, nb),
        in_specs=[
            pl.BlockSpec((hps * 2 * LANES, t), lambda h, i: (h, i)),
            pl.BlockSpec((nb, t, hps * LANES), lambda h, i: (0, 0, h)),
            pl.BlockSpec((nb, t, LANES), lambda h, i: (0, 0, 0), pipeline_mode=pl.Buffered(1)),
            pl.BlockSpec((nb, hps * LANES, t), lambda h, i: (0, h, 0)),
        ],
        out_specs=pl.BlockSpec((t, hps * LANES), lambda h, i: (i, h)),
        out_shape=jax.ShapeDtypeStruct((n, hw), BF16),
        scratch_shapes=[pltpu.VMEM((hps, 1, t), F32),
                        pltpu.VMEM((hps, LANES + SUM_ROWS, t), F32),
                        pltpu.VMEM((hps, t, t), F32), pltpu.VMEM((hps, t, t), F32),
                        pltpu.VMEM((hps, 1, t), F32), pltpu.VMEM((hps, 1, t), F32)],
        compiler_params=_params("arbitrary", "arbitrary"),
        name="attn_prompt",
    )(qt, kn3, kr3, vt3)


def _attn_s_kernel(qn_ref, qr_ref, latp_ref, krp_ref, latn_ref, krn_ref, wuk_ref, wuv_ref, o_ref,
                   *, past, ts, bps):
    rows = [slice(e * ts, (e + 1) * ts) for e in range(bps)]
    heads = [slice(h * LANES, (h + 1) * LANES) for h in range(MLA_HEADS)]
    lat = [jnp.concatenate([latp_ref[e].astype(BF16), latn_ref[r, :].astype(BF16)], axis=0)
           for e, r in enumerate(rows)]
    q_lat_h = [_dot_nt(qn_ref[:, sl], wuk_ref[:, sl]).astype(BF16) for sl in heads]
    q_lat = [jnp.concatenate([qh[r] for qh in q_lat_h], axis=0) for r in rows]
    q_rope = [jnp.concatenate([qr_ref[r, sl.start:sl.start + MLA_ROPE] for sl in heads], axis=0)
              for r in rows]
    s = [_dot_nt(q_lat[e], lat[e])
         + jnp.concatenate([_dot(q_rope[e], krp_ref[e].astype(BF16)),
                            _dot(q_rope[e], krn_ref[:, r].astype(BF16))], axis=1)
         for e, r in enumerate(rows)]
    if (past + ts - 1) // CHUNK > past // CHUNK:
        n_keys = past + ts
        qrow = lax.broadcasted_iota(jnp.int32, (MLA_HEADS * ts, n_keys), 0)
        keys = lax.broadcasted_iota(jnp.int32, (MLA_HEADS * ts, n_keys), 1)
        keep = keys // CHUNK <= (past + qrow % ts) // CHUNK
        s = [jnp.where(keep, si, NEG) for si in s]
    p = [jnp.exp2(si - jnp.max(si, axis=1, keepdims=True)) for si in s]
    l = [jnp.sum(pi, axis=1, keepdims=True) for pi in p]
    o_lat = [_dot(pi.astype(BF16), la) for pi, la in zip(p, lat)]
    o_lat = [(oi / li).astype(BF16) for oi, li in zip(o_lat, l)]
    for h, sl in enumerate(heads):
        o_h = jnp.concatenate([oi[h * ts:(h + 1) * ts] for oi in o_lat], axis=0)
        o_ref[:, sl] = _dot(o_h, wuv_ref[:, sl]).astype(BF16)


def _attn_sample(qn, qr, lat_past, kr_past, lat_new, kr_new, wuk, wuv, ts, bps):
    n, hw = qn.shape
    nb, past, _ = lat_past.shape
    new = lambda w: pl.BlockSpec((bps * ts, w), lambda b: (b, 0))
    old = lambda w: pl.BlockSpec((bps, past, w), lambda b: (b, 0, 0))
    old_t = pl.BlockSpec((bps, MLA_ROPE, past), lambda b: (b, 0, 0))
    new_t = pl.BlockSpec((MLA_ROPE, bps * ts), lambda b: (0, b))
    return pl.pallas_call(
        functools.partial(_attn_s_kernel, past=past, ts=ts, bps=bps),
        grid=(nb // bps,),
        in_specs=[new(hw), new(hw), old(MLA_KV_RANK), old_t, new(MLA_KV_RANK),
                  new_t, _const_spec(wuk.shape), _const_spec(wuv.shape)],
        out_specs=new(hw),
        out_shape=jax.ShapeDtypeStruct((n, hw), BF16),
        compiler_params=_params("arbitrary"),
        name="attn_sample",
    )(qn, qr, lat_past, kr_past, lat_new, kr_new, wuk, wuv)


def _post_kernel(x_ref, mix_ref, xq_ref, mk_ref, mv_ref, wo_ref, g1_ref, g2_ref, wup_ref,
                 wdn_ref, g3_ref, y_ref, *, nb, tb, interleaved):
    def mem_head(ref, b, h):
        if interleaved:
            return ref[b, pl.ds(h, ref.shape[1] // X_HEADS, stride=X_HEADS), :].astype(BF16)
        return ref[b, :, h * X_DIM:(h + 1) * X_DIM].astype(BF16)

    pairs = [(b, h) for b in range(nb) for h in range(X_HEADS)]
    s = [_dot_nt(xq_ref[b * tb:(b + 1) * tb, h * X_DIM:(h + 1) * X_DIM], mem_head(mk_ref, b, h))
         for b, h in pairs]
    p = [jnp.exp2(si - jnp.max(si, axis=1, keepdims=True)) for si in s]
    l = [jnp.sum(pi, axis=1, keepdims=True) for pi in p]
    o = [_dot(pi.astype(BF16), mem_head(mv_ref, b, h)) for pi, (b, h) in zip(p, pairs)]
    o = [(oi / li).astype(BF16) for oi, li in zip(o, l)]
    cross = [jnp.concatenate(o[b * X_HEADS:(b + 1) * X_HEADS], axis=1) for b in range(nb)]
    cross = cross[0] if nb == 1 else jnp.concatenate(cross, axis=0)
    cat = jnp.concatenate([mix_ref[...], cross], axis=1)
    x1 = x_ref[...] + _rms(_dot(cat, wo_ref[...]), g1_ref[...])

    hb = _rms(x1, g2_ref[...]).astype(BF16)
    d_ff = wup_ref.shape[1]
    step = 1024
    acc = None
    for c in range(d_ff // step):
        up = jnp.maximum(_dot(hb, wup_ref[:, c * step:(c + 1) * step]), 0.0)
        dn = _dot((up * up).astype(BF16), wdn_ref[c * step:(c + 1) * step, :])
        acc = dn if acc is None else acc + dn
    y_ref[...] = x1 + _rms(acc, g3_ref[...])


def _post(x, mix, xq, mk, mv, layer, wo, g1, g2, wup, wdn, g3, tm, tb):
    n, d = x.shape
    nb = tm // tb
    xw = X_HEADS * X_DIM
    _, nmem_b, mrows, mwidth = mk.shape
    interleaved = mwidth == X_DIM
    row = lambda w: pl.BlockSpec((tm, w), lambda i: (i, 0))
    if nmem_b * tb == n:
        mem = pl.BlockSpec((None, nb, mrows, mwidth), lambda i: (layer, i, 0, 0))
    else:
        assert nmem_b == 1 and nb == 1
        mem = pl.BlockSpec((None, 1, mrows, mwidth), lambda i: (layer, 0, 0, 0))
    return pl.pallas_call(
        functools.partial(_post_kernel, nb=nb, tb=tb, interleaved=interleaved),
        grid=(n // tm,),
        in_specs=[row(d), row(mix.shape[1]), row(xw), mem, mem, _const_spec(wo.shape),
                  _const_spec(g1.shape), _const_spec(g2.shape), _layer_spec(wup.shape, layer),
                  _layer_spec(wdn.shape, layer), _const_spec(g3.shape)],
        out_specs=row(d),
        out_shape=jax.ShapeDtypeStruct((n, d), F32),
        compiler_params=_params("arbitrary"),
        name="post",
    )(x, mix, xq, mk, mv, wo, g1, g2, wup, wdn, g3)


def _pre1_kernel(x_ref, g_ref, w_ref, lb_ref,
                 qs_ref, lf_ref, kk_ref, v_ref, gs_ref, xq_ref, *, layer):
    hb = _rms(x_ref[...], g_ref[...]).astype(BF16)
    w = HG_HEADS * HG_DIM
    lb_all = lb_ref[...]
    e = jnp.exp(lb_all - jnp.max(lb_all, axis=0, keepdims=True))
    soft = e / jnp.sum(e, axis=0, keepdims=True)
    lb = jnp.sum(soft[:layer + 1], axis=0, keepdims=True) - soft[0:1]
    q = _dot(hb, w_ref[:, :w])
    qs_ref[...] = q * _sigmoid(q)
    f = _dot(hb, w_ref[:, w:2 * w])
    lf_ref[...] = jnp.log(lb + (1.0 - lb) * _sigmoid(f))
    kk_ref[...] = (1.0 - lb) * _sigmoid(-f)
    v_ref[...] = _dot(hb, w_ref[:, 2 * w:3 * w]).astype(BF16)
    g = _dot(hb, w_ref[:, 3 * w:4 * w])
    gs_ref[...] = g * _sigmoid(g)
    xq_ref[...] = (_dot(hb, w_ref[:, 4 * w:]) * (X_DIM ** -0.5 * LOG2E)).astype(BF16)


def _pre1(x, g, w, lb, layer, tm):
    n, d = x.shape
    hw = HG_HEADS * HG_DIM
    xw = X_HEADS * X_DIM
    row = lambda wd: pl.BlockSpec((tm, wd), lambda i: (i, 0))
    f32o = jax.ShapeDtypeStruct((n, hw), F32)
    return pl.pallas_call(
        functools.partial(_pre1_kernel, layer=layer),
        grid=(n // tm,),
        in_specs=[row(d), _const_spec(g.shape), _const_spec(w.shape), _const_spec(lb.shape)],
        out_specs=[row(hw), row(hw), row(hw), row(hw), row(hw), row(xw)],
        out_shape=[f32o, f32o, f32o, jax.ShapeDtypeStruct((n, hw), BF16), f32o,
                   jax.ShapeDtypeStruct((n, xw), BF16)],
        compiler_params=_params("arbitrary"),
        name="pre1",
    )(x, g, w, lb)


def _split3(x):
    hi = x.astype(BF16)
    r1 = x - hi.astype(F32)
    mid = r1.astype(BF16)
    lo = (r1 - mid.astype(F32)).astype(BF16)
    return hi, mid, lo


def _gla_block(qs_ref, lf_ref, kk_ref, v_ref, gs_ref, gn, mix_ref, row0, cb, st=None,
               s0_ref=None, sfin_ref=None):
    t = cb * CHUNK
    d = HG_DIM
    independent = st is None
    row = lax.broadcasted_iota(jnp.int32, (t, t), 0)
    col = lax.broadcasted_iota(jnp.int32, (t, t), 1)
    causal = jnp.logical_and(row // CHUNK == col // CHUNK, col <= row)
    tri = jnp.where(causal, 1.0, 0.0).astype(BF16)
    hw = HG_HEADS * d
    heads = [slice(h * d, (h + 1) * d) for h in range(HG_HEADS)]

    bb = _dot(tri, jnp.concatenate(_split3(lf_ref[...]), axis=1))
    b = (bb[:, :hw] + bb[:, hw:2 * hw]) + bb[:, 2 * hw:]
    bcast = lambda r: jnp.concatenate(
        [jnp.broadcast_to(b[c * CHUNK + r:c * CHUNK + r + 1, :], (CHUNK, hw)) for c in range(cb)],
        axis=0)
    ref = bcast(CHUNK // 2)
    last = bcast(CHUNK - 1)
    qs = qs_ref[...]
    kk = kk_ref[...]
    v = v_ref[...]
    qe = (qs * jnp.exp(b - ref)).astype(BF16)
    ke = (kk * jnp.exp(ref - b)).astype(BF16)
    kh = (kk * jnp.exp(last - b)).astype(BF16)
    qb = (qs * jnp.exp(b)).astype(BF16)
    a = [jnp.where(causal, _dot_nt(qe[:, sl], ke[:, sl]), 0.0).astype(BF16) for sl in heads]
    o = [_dot(a[h], v[:, sl]) for h, sl in enumerate(heads)]
    inter = [[] for _ in heads]
    for c in range(cb):
        rows = slice(c * CHUNK, (c + 1) * CHUNK)
        decay = jnp.exp(b[c * CHUNK + CHUNK - 1:c * CHUNK + CHUNK, :])
        if independent:
            st = [s0_ref[c, h].T for h in range(HG_HEADS)]
        for h, sl in enumerate(heads):
            inter[h].append(_dot_nt(qb[rows, sl], st[h].astype(BF16)))
        st = [st[h] * decay[:, sl] + _dot_tn(v[rows, sl], kh[rows, sl])
              for h, sl in enumerate(heads)]
        if independent:
            for h in range(HG_HEADS):
                sfin_ref[c, h] = st[h].T
    for h, sl in enumerate(heads):
        o_h = o[h] + (inter[h][0] if cb == 1 else jnp.concatenate(inter[h], axis=0))
        mix_ref[row0:row0 + t, sl] = (_rms(o_h, gn) * gs_ref[:, sl]).astype(BF16)
    return st


def _gla_kernel(qs_ref, lf_ref, kk_ref, v_ref, gs_ref, s0_ref, gn_ref, mix_ref, sfin_ref, *, cb):
    _gla_block(qs_ref, lf_ref, kk_ref, v_ref, gs_ref, gn_ref[...], mix_ref, 0, cb,
               s0_ref=s0_ref, sfin_ref=sfin_ref)


def _gla(qs, lf, kk, v, gs, s0, gn, cb):
    n, hw = qs.shape
    t = cb * CHUNK
    row = pl.BlockSpec((t, hw), lambda b: (b, 0))
    st = pl.BlockSpec((cb, HG_HEADS, HG_DIM, HG_DIM), lambda b: (b, 0, 0, 0))
    return pl.pallas_call(
        functools.partial(_gla_kernel, cb=cb),
        grid=(n // t,),
        in_specs=[row, row, row, row, row, st, _const_spec(gn.shape)],
        out_specs=[row, st],
        out_shape=[jax.ShapeDtypeStruct((n, hw), BF16),
                   jax.ShapeDtypeStruct(s0.shape, F32)],
        compiler_params=_params("arbitrary"),
        name="gla",
    )(qs, lf, kk, v, gs, s0, gn)


def _hgrn_seq_kernel(x0_ref, xa_ref, xb_ref, g_ref, w_ref, lb_ref, gn_ref, s0_ref,
                     mix_ref, xq_ref, sfin_ref, *scratch, layer, cb):
    slot_a, slot_b, st_ref = scratch[:6], scratch[6:12], scratch[12]
    i = pl.program_id(0)
    t = cb * CHUNK
    project = functools.partial(_pre1_kernel, layer=layer)

    @pl.when(i == 0)
    def _():
        project(x0_ref, g_ref, w_ref, lb_ref, *slot_a)
        for h in range(HG_HEADS):
            st_ref[h] = s0_ref[0, h].T

    gn = gn_ref[...]
    st = [st_ref[h] for h in range(HG_HEADS)]
    for half, (cur, nxt, x_ref) in enumerate(((slot_a, slot_b, xa_ref), (slot_b, slot_a, xb_ref))):
        project(x_ref, g_ref, w_ref, lb_ref, *nxt)
        st = _gla_block(*cur[:5], gn, mix_ref, half * t, cb, st=st)
        xq_ref[half * t:(half + 1) * t, :] = cur[5][...]
    for h in range(HG_HEADS):
        st_ref[h] = st[h]

    @pl.when(i == pl.num_programs(0) - 1)
    def _():
        for h in range(HG_HEADS):
            sfin_ref[0, h] = st_ref[h].T


def _hgrn_seq(x, g, w, lb, layer, s0, gn, cb):
    n, d = x.shape
    hw = HG_HEADS * HG_DIM
    xw = X_HEADS * X_DIM
    t = cb * CHUNK
    nblk = n // t
    assert nblk % 2 == 0 and s0.shape[0] == 1
    xspec = lambda f: pl.BlockSpec((t, d), lambda i: (f(i), 0))
    out = lambda wd: pl.BlockSpec((2 * t, wd), lambda i: (i, 0))
    st = pl.BlockSpec((1, HG_HEADS, HG_DIM, HG_DIM), lambda i: (0, 0, 0, 0))
    slot = [pltpu.VMEM((t, hw), F32), pltpu.VMEM((t, hw), F32), pltpu.VMEM((t, hw), F32),
            pltpu.VMEM((t, hw), BF16), pltpu.VMEM((t, hw), F32), pltpu.VMEM((t, xw), BF16)]
    return pl.pallas_call(
        functools.partial(_hgrn_seq_kernel, layer=layer, cb=cb),
        grid=(nblk // 2,),
        in_specs=[xspec(lambda i: 0), xspec(lambda i: 2 * i + 1),
                  xspec(lambda i: jnp.minimum(2 * i + 2, nblk - 1)),
                  _const_spec(g.shape), _const_spec(w.shape), _const_spec(lb.shape),
                  _const_spec(gn.shape), st],
        out_specs=[out(hw), out(xw), st],
        out_shape=[jax.ShapeDtypeStruct((n, hw), BF16), jax.ShapeDtypeStruct((n, xw), BF16),
                   jax.ShapeDtypeStruct(s0.shape, F32)],
        scratch_shapes=slot + slot + [pltpu.VMEM((HG_HEADS, HG_DIM, HG_DIM), F32)],
        compiler_params=_params("arbitrary"),
        name="hgrn_seq",
    )(x, x, x, g, w, lb, gn, s0)


def _rope_tables(tile_pos, tile_offsets):
    half = MLA_ROPE // 2
    inv = jnp.power(ROPE_THETA, -jnp.arange(half, dtype=F32) / half)

    def cos_sin(pos):
        ang = pos.astype(F32)[:, None] * inv[None, :]
        return jnp.cos(ang), jnp.sin(ang)

    pat = lambda a: jnp.concatenate([a, a, jnp.zeros_like(a), jnp.zeros_like(a)], axis=1)
    cos, sin = cos_sin(tile_pos)
    cos0, sin0 = cos_sin(tile_offsets)
    wide = lambda a: jnp.broadcast_to(a[:, :, None], a.shape + (LANES,))
    return (pat(cos), pat(sin), pat(cos0)[:, None, :], pat(sin0)[:, None, :],
            cos.T, sin.T, wide(cos0), wide(sin0))


def _prep_weights(mla_w_in, mla_w_uq, mla_w_uk, mla_w_uv):
    o1 = MLA_Q_RANK + MLA_KV_RANK
    o2 = o1 + MLA_ROPE
    d = mla_w_in.shape[0]
    win = jnp.concatenate([mla_w_in[:, :o1], mla_w_in[:, o2:], mla_w_in[:, o1:o2],
                           jnp.zeros((d, LANES - MLA_ROPE), mla_w_in.dtype)], axis=1)
    wq = mla_w_uq.reshape(MLA_Q_RANK, MLA_HEADS, MLA_NOPE + MLA_ROPE)
    nope = wq[:, :, :MLA_NOPE].reshape(MLA_Q_RANK, MLA_HEADS * MLA_NOPE)
    rope = jnp.pad(wq[:, :, MLA_NOPE:], ((0, 0), (0, 0), (0, LANES - MLA_ROPE)))
    wuq = jnp.concatenate([nope, rope.reshape(MLA_Q_RANK, MLA_HEADS * LANES)], axis=1)
    wuk = mla_w_uk.reshape(MLA_KV_RANK, MLA_HEADS * MLA_NOPE)
    wuv = mla_w_uv.reshape(MLA_KV_RANK, MLA_HEADS * MLA_V)
    wq_t = jnp.pad(wq, ((0, 0), (0, 0), (0, LANES - MLA_ROPE))).reshape(MLA_Q_RANK, -1).T
    bf = lambda a: a.astype(BF16)
    return bf(win), bf(wuq), bf(wuk), bf(wuv), bf(wq_t), bf(wuv.T)


def kernel(x_prompt, x_sample, cache_mla_latent, cache_mla_krope, cache_hgrn_state, cache_mem_k,
           cache_mem_v, mem_prompt, ln_mix_pre, ln_mix_post, ln_ffn_pre, ln_ffn_post, mem_norm,
           w_mem_kv, mla_w_in, mla_q_norm, mla_kv_norm, mla_w_uq, mla_w_uk, mla_w_uv, mla_w_out,
           hgrn_w_in, hgrn_lb, hgrn_o_norm, hgrn_w_out, w_ffn_up, w_ffn_down):
    bp, tp, d = x_prompt.shape
    bs, ts, _ = x_sample.shape
    depth = ln_mix_pre.shape[0]
    past = cache_mla_latent.shape[2]
    n_mem = mem_prompt.shape[1]
    xw = X_HEADS * X_DIM
    assert depth == 2 and bp == 1 and ts == CHUNK and past % CHUNK == 0
    assert MLA_NOPE == LANES and MLA_V == LANES and HG_DIM == LANES and X_DIM == LANES

    row2 = lambda a: a.reshape(1, -1)
    win0, wuq, wuk, wuv, wuq_t, wuv_t = _prep_weights(mla_w_in[0], mla_w_uq[0], mla_w_uk[0],
                                                      mla_w_uv[0])
    wout = (mla_w_out[0].astype(BF16), hgrn_w_out[0].astype(BF16))
    wup = w_ffn_up.astype(BF16)
    wdn = w_ffn_down.astype(BF16)
    win1 = hgrn_w_in[0].astype(BF16)

    mem_k_p, mem_v_p = _memkv(mem_prompt, mem_norm, w_mem_kv.astype(BF16))

    def trunk(x, pos, mem_k, mem_v, s0, nbatch, tm, tb, history):
        n = x.shape[0]
        tables = _rope_tables(*pos)
        pre0_args = (x, row2(ln_mix_pre[0]), win0, row2(mla_q_norm[0]), row2(mla_kv_norm[0]))
        if history is None:
            qt, kn, kr, vt3, lat, krope, xq = _pre0(
                *pre0_args, wuq_t, tables, ATTN_TILE, (wuk, wuv_t))
            mix = _attn_prompt(qt, kn, kr, vt3, ATTN_TILE, ATTN_HEADS_PER_STEP)
        else:
            qn, qr, lat, krope, xq = _pre0(*pre0_args, wuq, tables, tm)
            mix = _attn_sample(qn, qr, history[0], history[1], lat, krope, wuk, wuv, ts,
                               SAMPLE_ATTN_BATCH)
        x = _post(x, mix, xq, mem_k, mem_v, 0, wout[0],
                  row2(ln_mix_post[0]), row2(ln_ffn_pre[0]), wup, wdn,
                  row2(ln_ffn_post[0]), tm, tb)
        if n // nbatch == CHUNK:
            qs, lf, kk, vv, gs, xq = _pre1(x, row2(ln_mix_pre[1]), win1, hgrn_lb, 1, PRE1_TILE)
            mix, s_fin = _gla(qs, lf, kk, vv, gs, s0, row2(hgrn_o_norm[0]), GLA_CHUNKS)
        else:
            mix, xq, s_fin = _hgrn_seq(x, row2(ln_mix_pre[1]), win1, hgrn_lb, 1, s0,
                                       row2(hgrn_o_norm[0]), GLA_CHUNKS)
        x = _post(x, mix, xq, mem_k, mem_v, 1, wout[1],
                  row2(ln_mix_post[1]), row2(ln_ffn_pre[1]), wup, wdn,
                  row2(ln_ffn_post[1]), tm, tb)
        return x, lat, krope, s_fin

    s0_p = jnp.zeros((bp, HG_HEADS, HG_DIM, HG_DIM), F32)
    pos_p = (jnp.arange(ATTN_TILE), ATTN_TILE * jnp.arange(bp * tp // ATTN_TILE))
    y_p, lat_p, kr_p, st_p = trunk(x_prompt.reshape(bp * tp, d), pos_p, mem_k_p, mem_v_p,
                                   s0_p, bp, TOKEN_TILE, TOKEN_TILE, None)
    pos_s = (jnp.tile(past + jnp.arange(ts), TOKEN_TILE // ts),
             jnp.zeros((bs * ts // TOKEN_TILE,), jnp.int32))
    mk_s = cache_mem_k.reshape(depth, bs, n_mem * X_HEADS, X_DIM)
    mv_s = cache_mem_v.reshape(depth, bs, n_mem * X_HEADS, X_DIM)
    y_s, lat_s, kr_s, st_s = trunk(x_sample.reshape(bs * ts, d), pos_s, mk_s, mv_s,
                                   cache_hgrn_state[0], bs, TOKEN_TILE, ts,
                                   (cache_mla_latent[0], jnp.swapaxes(cache_mla_krope[0], 1, 2)))

    return (y_p.reshape(bp, tp, d), y_s.reshape(bs, ts, d),
            lat_p.reshape(1, bp, tp, MLA_KV_RANK), kr_p.T.reshape(1, bp, tp, MLA_ROPE),
            st_p.reshape(1, bp, HG_HEADS, HG_DIM, HG_DIM),
            mem_k_p.reshape(depth, bp, n_mem, X_HEADS, X_DIM),
            mem_v_p.reshape(depth, bp, n_mem, X_HEADS, X_DIM),
            lat_s.reshape(1, bs, ts, MLA_KV_RANK), kr_s.T.reshape(1, bs, ts, MLA_ROPE),
            st_s.reshape(1, bs, HG_HEADS, HG_DIM, HG_DIM))
```

```python
import functools
import itertools

import jax
import jax.numpy as jnp
from jax import lax
from jax.experimental import pallas as pl
from jax.experimental.pallas import tpu as pltpu

F32 = jnp.float32
BF16 = jnp.bfloat16

CHUNK = 64
EPS = 1e-6
ROPE_THETA = 10000.0
NEG = -1e30
MLA_HEADS = 8
MLA_NOPE = 128
MLA_ROPE = 64
MLA_V = 128
MLA_Q_RANK = 384
MLA_KV_RANK = 256
HG_HEADS = 8
HG_DIM = 128
X_HEADS = 4
X_DIM = 128
LANES = 128
LOG2E = 1.4426950408889634

VMEM_LIMIT_BYTES = 52 * 1024 * 1024

TOKEN_TILE = 512
PRE1_TILE = 256
STREAM_BUFFERS = 3
FFN_CHUNK = 1024
ATTN_TILE = 512
ATTN_HEADS_PER_STEP = 2
GLA_CHUNKS = 4
HGRN_BLOCKS_PER_STEP = 4
SAMPLE_ATTN_BATCH = 4
Q_ROWS = MLA_NOPE + MLA_ROPE
LOOP_PAIRS = 4
SUM_ROWS = 16


def _dot(a, b):
    return jnp.dot(a, b, preferred_element_type=F32)


def _dot_nt(a, b):
    return lax.dot_general(a, b, (((1,), (1,)), ((), ())), preferred_element_type=F32)


def _dot_tn(a, b):
    return lax.dot_general(a, b, (((0,), (0,)), ((), ())), preferred_element_type=F32)


def _rms(x, g):
    ms = jnp.mean(x * x, axis=-1, keepdims=True)
    return x * lax.rsqrt(ms + EPS) * g


def _sigmoid(x):
    return 1.0 / (1.0 + jnp.exp(-x))


def _params(*sem):
    return pltpu.CompilerParams(dimension_semantics=sem, vmem_limit_bytes=VMEM_LIMIT_BYTES)


def _const_spec(shape):
    nd = len(shape)
    return pl.BlockSpec(shape, lambda *_: (0,) * nd, pipeline_mode=pl.Buffered(1))


def _layer_spec(shape, layer):
    nd = len(shape) - 1
    return pl.BlockSpec((None,) + tuple(shape[1:]), lambda *_: (layer,) + (0,) * nd,
                        pipeline_mode=pl.Buffered(1))


def _memkv_kernel(mem_ref, g_ref, w_ref, k_ref, v_ref):
    h = _rms(mem_ref[0], g_ref[0]).astype(BF16)
    kv = _dot(h, w_ref[0])
    half = kv.shape[1] // 2
    k_ref[0, 0] = kv[:, :half]
    v_ref[0, 0] = kv[:, half:]


def _memkv(mem, g, w):
    bp, n_mem, d = mem.shape
    depth = g.shape[0]
    width = w.shape[2] // 2
    out = jax.ShapeDtypeStruct((depth, bp, n_mem, width), F32)
    return pl.pallas_call(
        _memkv_kernel,
        grid=(depth, bp),
        in_specs=[
            pl.BlockSpec((1, n_mem, d), lambda l, b: (b, 0, 0)),
            pl.BlockSpec((1, 1, d), lambda l, b: (l, 0, 0)),
            pl.BlockSpec((1, d, 2 * width), lambda l, b: (l, 0, 0)),
        ],
        out_specs=[
            pl.BlockSpec((1, 1, n_mem, width), lambda l, b: (l, b, 0, 0)),
            pl.BlockSpec((1, 1, n_mem, width), lambda l, b: (l, b, 0, 0)),
        ],
        out_shape=[out, out],
        compiler_params=_params("arbitrary", "arbitrary"),
        name="memkv",
    )(mem, g.reshape(depth, 1, d), w)


def _rope_rot(x, c, s1, s2):
    w = x.shape[1]
    half = MLA_ROPE // 2
    return x * c + pltpu.roll(x, half, 1) * s1 + pltpu.roll(x, w - half, 1) * s2


def _pre0_kernel(*refs, expand):
    (x_ref, g_ref, win_ref, qg_ref, kvg_ref, wq_ref,
     cos_ref, sin_ref, cos0_ref, sin0_ref) = refs[:10]
    if expand:
        (wuk_ref, wv_ref, cos_t_ref, sin_t_ref, cos0_t_ref, sin0_t_ref,
         q_ref, kn_ref, kr_ref, v_ref, lat_ref, krope_ref, xq_ref) = refs[10:]
    else:
        qn_ref, qr_ref, lat_ref, krope_ref, xq_ref = refs[10:]
    h = _rms(x_ref[...], g_ref[...]).astype(BF16)
    proj = _dot(h, win_ref[...])
    o1 = MLA_Q_RANK
    o2 = o1 + MLA_KV_RANK
    o3 = o2 + X_HEADS * X_DIM
    c_q, c_kv, xq, k_r = proj[:, :o1], proj[:, o1:o2], proj[:, o2:o3], proj[:, o3:]

    scale = (MLA_NOPE + MLA_ROPE) ** -0.5 * LOG2E
    qc = _rms(c_q, qg_ref[...]).astype(BF16)
    half = MLA_ROPE // 2
    cos0, sin0 = cos0_ref[0], sin0_ref[0]
    c = cos0 * cos_ref[...] - sin0 * sin_ref[...]
    sn = sin0 * cos_ref[...] + cos0 * sin_ref[...]
    lane = lax.broadcasted_iota(jnp.int32, sn.shape, 1)
    s1 = jnp.where(lane >= half, sn, 0.0)
    s2 = jnp.where(lane < half, -sn, 0.0)
    lat = _rms(c_kv, kvg_ref[...])
    lat_ref[...] = lat
    kr = _rope_rot(k_r, c, s1, s2)
    krope_ref[...] = kr.T[:MLA_ROPE, :]
    xq_ref[...] = (xq * (X_DIM ** -0.5 * LOG2E)).astype(BF16)
    if expand:
        latb = lat.astype(BF16)
        kn_ref[...] = _dot(latb, wuk_ref[...]).astype(BF16)
        kr_ref[...] = kr.astype(BF16)
        qt = _dot_nt(wq_ref[...], qc) * scale
        lanes4 = lambda a: jnp.concatenate([a] * (qt.shape[1] // LANES), axis=1)
        cos0_t, sin0_t = lanes4(cos0_t_ref[0]), lanes4(sin0_t_ref[0])
        cos = cos0_t * cos_t_ref[...] - sin0_t * sin_t_ref[...]
        sin = sin0_t * cos_t_ref[...] + cos0_t * sin_t_ref[...]
        slabs = []
        for hd in range(MLA_HEADS):
            r0 = hd * Q_ROWS + MLA_NOPE
            x1, x2 = qt[r0:r0 + half], qt[r0 + half:r0 + 2 * half]
            slabs += [qt[hd * Q_ROWS:r0], x1 * cos - x2 * sin, x2 * cos + x1 * sin]
        q_ref[...] = jnp.concatenate(slabs, axis=0).astype(BF16)
        v_ref[0] = _dot_nt(wv_ref[...], latb).astype(BF16)
    else:
        q = _dot(qc, wq_ref[...]) * scale
        nope_w = MLA_HEADS * MLA_NOPE
        qn_ref[...] = q[:, :nope_w].astype(BF16)
        tile = lambda t: jnp.concatenate([t] * MLA_HEADS, axis=1)
        qr_ref[...] = _rope_rot(q[:, nope_w:], tile(c), tile(s1), tile(s2)).astype(BF16)


def _pre0(x, g, win, qg, kvg, wq, tables, tm, expand_weights=None):
    n, d = x.shape
    hw = MLA_HEADS * LANES
    xw = X_HEADS * X_DIM
    expand = expand_weights is not None
    row = lambda w: pl.BlockSpec((tm, w), lambda i: (i, 0))
    col = lambda r: pl.BlockSpec((r, tm), lambda i: (0, i))
    sds = jax.ShapeDtypeStruct
    deep = pl.BlockSpec((tm, d), lambda i: (i, 0), pipeline_mode=pl.Buffered(STREAM_BUFFERS))
    in_specs = [deep, _const_spec(g.shape), _const_spec(win.shape), _const_spec(qg.shape),
                _const_spec(kvg.shape), _const_spec(wq.shape),
                _const_spec((tm, LANES)), _const_spec((tm, LANES)),
                pl.BlockSpec((1, 1, LANES), lambda i: (i, 0, 0)),
                pl.BlockSpec((1, 1, LANES), lambda i: (i, 0, 0))]
    common_specs = [row(MLA_KV_RANK), col(MLA_ROPE), row(xw)]
    common_shapes = [sds((n, MLA_KV_RANK), F32),
                     sds((MLA_ROPE, n), F32),
                     sds((n, xw), BF16)]
    if expand:
        half = MLA_ROPE // 2
        in_specs += [_const_spec(w.shape) for w in expand_weights]
        in_specs += [_const_spec((half, tm)), _const_spec((half, tm)),
                     pl.BlockSpec((1, half, LANES), lambda i: (i, 0, 0)),
                     pl.BlockSpec((1, half, LANES), lambda i: (i, 0, 0))]
        qw = MLA_HEADS * Q_ROWS
        out_specs = [col(qw), row(hw), row(LANES), pl.BlockSpec((1, hw, tm), lambda i: (i, 0, 0))]
        out_shape = [sds((qw, n), BF16),
                     sds((n, hw), BF16),
                     sds((n, LANES), BF16),
                     sds((n // tm, hw, tm), BF16)]
        operands = (x, g, win, qg, kvg, wq) + tuple(tables[:4]) + tuple(expand_weights) + tuple(tables[4:])
    else:
        out_specs = [row(hw), row(hw)]
        out_shape = [sds((n, hw), BF16),
                     sds((n, hw), BF16)]
        operands = (x, g, win, qg, kvg, wq) + tuple(tables[:4])
    all_out_specs = out_specs + common_specs

    def streamed(*refs):
        pltpu.emit_pipeline(functools.partial(_pre0_kernel, expand=expand), grid=(n // tm,),
                            in_specs=in_specs, out_specs=all_out_specs)(*refs)

    anywhere = pl.BlockSpec(memory_space=pl.ANY)
    return pl.pallas_call(
        streamed,
        in_specs=[anywhere] * len(in_specs),
        out_specs=[anywhere] * len(all_out_specs),
        out_shape=out_shape + common_shapes,
        compiler_params=pltpu.CompilerParams(vmem_limit_bytes=VMEM_LIMIT_BYTES),
        name="pre0",
    )(*operands)


def _attn_p_kernel(q_ref, kn_ref, kr_ref, v_ref, o_ref, m_ref, acc_ref, sa_ref, sb_ref,
                   mxa_ref, mxb_ref, *, t, hps):
    qi = pl.program_id(1)
    m_ref[...] = jnp.full(m_ref.shape, NEG, F32)
    acc_ref[...] = jnp.zeros(acc_ref.shape, F32)
    ones = jnp.ones((SUM_ROWS, t), BF16)

    q_pad = jnp.zeros((2 * LANES - Q_ROWS, t), BF16)

    def scores(j, s_ref, mx_ref):
        kr = kr_ref[j]
        for hh in range(hps):
            k_blk = jnp.concatenate([kn_ref[j, :, hh * LANES:(hh + 1) * LANES], kr], axis=1)
            q_h = jnp.concatenate([q_ref[hh * Q_ROWS:(hh + 1) * Q_ROWS, :], q_pad], axis=0)
            s = _dot(k_blk, q_h)
            s_ref[hh] = s
            mx_ref[hh] = jnp.max(s, axis=0, keepdims=True)

    def update(j, s_ref, mx_ref, masked):
        if masked:
            keep = (lax.broadcasted_iota(jnp.int32, (t, t), 0) // CHUNK
                    <= lax.broadcasted_iota(jnp.int32, (t, t), 1) // CHUNK)
        for hh in range(hps):
            s = s_ref[hh]
            if masked:
                s = jnp.where(keep, s, NEG)
                m_blk = jnp.max(s, axis=0, keepdims=True)
            else:
                m_blk = mx_ref[hh]
            m_prev = m_ref[hh]
            m_new = jnp.maximum(m_prev, m_blk)
            alpha = jnp.exp2(m_prev - m_new)
            p = jnp.exp2(s - m_new).astype(BF16)
            v_aug = jnp.concatenate([v_ref[j, hh * LANES:(hh + 1) * LANES, :], ones], axis=0)
            acc_ref[hh] = alpha * acc_ref[hh] + _dot(v_aug, p)
            m_ref[hh] = m_new

    scores(0, sa_ref, mxa_ref)

    def pair(j):
        scores(j + 1, sb_ref, mxb_ref)
        update(j, sa_ref, mxa_ref, False)
        scores(j + 2, sa_ref, mxa_ref)
        update(j + 1, sb_ref, mxb_ref, False)

    def pairs(i, _):
        for u in range(LOOP_PAIRS):
            pair(2 * (LOOP_PAIRS * i + u))
        return 0

    lax.fori_loop(0, qi // (2 * LOOP_PAIRS), pairs, 0)
    done = (qi // (2 * LOOP_PAIRS)) * LOOP_PAIRS
    for u in range(LOOP_PAIRS - 1):
        @pl.when(qi // 2 - done > u)
        def _():
            pair(2 * (done + u))

    @pl.when(qi % 2 == 0)
    def _():
        update(qi, sa_ref, mxa_ref, True)

    @pl.when(qi % 2 == 1)
    def _():
        scores(qi, sb_ref, mxb_ref)
        update(qi - 1, sa_ref, mxa_ref, False)
        update(qi, sb_ref, mxb_ref, True)

    for hh in range(hps):
        o = acc_ref[hh, :LANES, :] / acc_ref[hh, LANES:LANES + 1, :]
        o_ref[:, hh * LANES:(hh + 1) * LANES] = o.T.astype(BF16)


def _attn_prompt(qt, kn, kr, vt3, t, hps):
    n, hw = kn.shape
    nb = n // t
    kn3 = kn.reshape(nb, t, hw)
    kr3 = kr.reshape(nb, t, LANES)
    return pl.pallas_call(
        functools.partial(_attn_p_kernel, t=t, hps=hps),
        grid=(MLA_HEADS // hps, nb),
        in_specs=[
            pl.BlockSpec((hps * Q_ROWS, t), lambda h, i: (h, i)),
            pl.BlockSpec((nb, t, hps * LANES), lambda h, i: (0, 0, h)),
            pl.BlockSpec((nb, t, LANES), lambda h, i: (0, 0, 0), pipeline_mode=pl.Buffered(1)),
            pl.BlockSpec((nb, hps * LANES, t), lambda h, i: (0, h, 0)),
        ],
        out_specs=pl.BlockSpec((t, hps * LANES), lambda h, i: (i, h)),
        out_shape=jax.ShapeDtypeStruct((n, hw), BF16),
        scratch_shapes=[pltpu.VMEM((hps, 1, t), F32),
                        pltpu.VMEM((hps, LANES + SUM_ROWS, t), F32),
                        pltpu.VMEM((hps, t, t), F32), pltpu.VMEM((hps, t, t), F32),
                        pltpu.VMEM((hps, 1, t), F32), pltpu.VMEM((hps, 1, t), F32)],
        compiler_params=_params("arbitrary", "arbitrary"),
        name="attn_prompt",
    )(qt, kn3, kr3, vt3)


def _attn_s_kernel(qn_ref, qr_ref, latp_ref, krp_ref, latn_ref, krn_ref, wuk_ref, wuv_ref, o_ref,
                   *, past, ts, bps):
    rows = [slice(e * ts, (e + 1) * ts) for e in range(bps)]
    heads = [slice(h * LANES, (h + 1) * LANES) for h in range(MLA_HEADS)]
    lat = [jnp.concatenate([latp_ref[e].astype(BF16), latn_ref[r, :].astype(BF16)], axis=0)
           for e, r in enumerate(rows)]
    q_lat_h = [_dot_nt(qn_ref[:, sl], wuk_ref[:, sl]).astype(BF16) for sl in heads]
    q_lat = [jnp.concatenate([qh[r] for qh in q_lat_h], axis=0) for r in rows]
    q_rope = [jnp.concatenate([qr_ref[r, sl.start:sl.start + MLA_ROPE] for sl in heads], axis=0)
              for r in rows]
    s = [_dot_nt(q_lat[e], lat[e])
         + jnp.concatenate([_dot(q_rope[e], krp_ref[e].astype(BF16)),
                            _dot(q_rope[e], krn_ref[:, r].astype(BF16))], axis=1)
         for e, r in enumerate(rows)]
    if (past + ts - 1) // CHUNK > past // CHUNK:
        n_keys = past + ts
        qrow = lax.broadcasted_iota(jnp.int32, (MLA_HEADS * ts, n_keys), 0)
        keys = lax.broadcasted_iota(jnp.int32, (MLA_HEADS * ts, n_keys), 1)
        keep = keys // CHUNK <= (past + qrow % ts) // CHUNK
        s = [jnp.where(keep, si, NEG) for si in s]
    p = [jnp.exp2(si - jnp.max(si, axis=1, keepdims=True)) for si in s]
    l = [jnp.sum(pi, axis=1, keepdims=True) for pi in p]
    o_lat = [_dot(pi.astype(BF16), la) for pi, la in zip(p, lat)]
    o_lat = [(oi / li).astype(BF16) for oi, li in zip(o_lat, l)]
    for h, sl in enumerate(heads):
        o_h = jnp.concatenate([oi[h * ts:(h + 1) * ts] for oi in o_lat], axis=0)
        o_ref[:, sl] = _dot(o_h, wuv_ref[:, sl]).astype(BF16)


def _attn_sample(qn, qr, lat_past, kr_past, lat_new, kr_new, wuk, wuv, ts, bps):
    n, hw = qn.shape
    nb, past, _ = lat_past.shape
    new = lambda w: pl.BlockSpec((bps * ts, w), lambda b: (b, 0))
    old = lambda w: pl.BlockSpec((bps, past, w), lambda b: (b, 0, 0))
    old_t = pl.BlockSpec((bps, MLA_ROPE, past), lambda b: (b, 0, 0))
    new_t = pl.BlockSpec((MLA_ROPE, bps * ts), lambda b: (0, b))
    return pl.pallas_call(
        functools.partial(_attn_s_kernel, past=past, ts=ts, bps=bps),
        grid=(nb // bps,),
        in_specs=[new(hw), new(hw), old(MLA_KV_RANK), old_t, new(MLA_KV_RANK),
                  new_t, _const_spec(wuk.shape), _const_spec(wuv.shape)],
        out_specs=new(hw),
        out_shape=jax.ShapeDtypeStruct((n, hw), BF16),
        compiler_params=_params("arbitrary"),
        name="attn_sample",
    )(qn, qr, lat_past, kr_past, lat_new, kr_new, wuk, wuv)


def _post_kernel(x_ref, mix_ref, xq_ref, mk_ref, mv_ref, wo_ref, g1_ref, g2_ref, wup_ref,
                 wdn_ref, g3_ref, y_ref, *, nb, tb, interleaved):
    def mem_head(ref, b, h):
        if interleaved:
            return ref[b, pl.ds(h, ref.shape[1] // X_HEADS, stride=X_HEADS), :].astype(BF16)
        return ref[b, :, h * X_DIM:(h + 1) * X_DIM].astype(BF16)

    pairs = [(b, h) for b in range(nb) for h in range(X_HEADS)]
    s = [_dot_nt(xq_ref[b * tb:(b + 1) * tb, h * X_DIM:(h + 1) * X_DIM], mem_head(mk_ref, b, h))
         for b, h in pairs]
    p = [jnp.exp2(si - jnp.max(si, axis=1, keepdims=True)) for si in s]
    l = [jnp.sum(pi, axis=1, keepdims=True) for pi in p]
    o = [_dot(pi.astype(BF16), mem_head(mv_ref, b, h)) for pi, (b, h) in zip(p, pairs)]
    o = [(oi / li).astype(BF16) for oi, li in zip(o, l)]
    cross = [jnp.concatenate(o[b * X_HEADS:(b + 1) * X_HEADS], axis=1) for b in range(nb)]
    cross = cross[0] if nb == 1 else jnp.concatenate(cross, axis=0)
    cat = jnp.concatenate([mix_ref[...], cross], axis=1)
    x1 = x_ref[...] + _rms(_dot(cat, wo_ref[...]), g1_ref[...])

    hb = _rms(x1, g2_ref[...]).astype(BF16)
    d_ff = wup_ref.shape[1]
    step = FFN_CHUNK
    acc = None
    for c in range(d_ff // step):
        up = jnp.maximum(_dot(hb, wup_ref[:, c * step:(c + 1) * step]), 0.0)
        dn = _dot((up * up).astype(BF16), wdn_ref[c * step:(c + 1) * step, :])
        acc = dn if acc is None else acc + dn
    y_ref[...] = x1 + _rms(acc, g3_ref[...])


def _post(x, mix, xq, mk, mv, layer, wo, g1, g2, wup, wdn, g3, tm, tb):
    n, d = x.shape
    nb = tm // tb
    xw = X_HEADS * X_DIM
    _, nmem_b, mrows, mwidth = mk.shape
    interleaved = mwidth == X_DIM
    row = lambda w: pl.BlockSpec((tm, w), lambda i: (i, 0))
    if nmem_b * tb == n:
        mem = pl.BlockSpec((None, nb, mrows, mwidth), lambda i: (layer, i, 0, 0))
    else:
        assert nmem_b == 1 and nb == 1
        mem = pl.BlockSpec((None, 1, mrows, mwidth), lambda i: (layer, 0, 0, 0))
    return pl.pallas_call(
        functools.partial(_post_kernel, nb=nb, tb=tb, interleaved=interleaved),
        grid=(n // tm,),
        in_specs=[row(d), row(mix.shape[1]), row(xw), mem, mem, _const_spec(wo.shape),
                  _const_spec(g1.shape), _const_spec(g2.shape), _layer_spec(wup.shape, layer),
                  _layer_spec(wdn.shape, layer), _const_spec(g3.shape)],
        out_specs=row(d),
        out_shape=jax.ShapeDtypeStruct((n, d), F32),
        compiler_params=_params("arbitrary"),
        name="post",
    )(x, mix, xq, mk, mv, wo, g1, g2, wup, wdn, g3)


def _gate_steps(x_ref, g_ref, w_ref, lb_ref, qs_ref, lf_ref, kk_ref, v_ref, gs_ref, xq_ref, layer):
    hb = _rms(x_ref[...], g_ref[...]).astype(BF16)
    w = HG_HEADS * HG_DIM
    lb_all = lb_ref[...]
    e = jnp.exp(lb_all - jnp.max(lb_all, axis=0, keepdims=True))
    soft = e / jnp.sum(e, axis=0, keepdims=True)
    lb = jnp.sum(soft[:layer + 1], axis=0, keepdims=True) - soft[0:1]
    yield
    q = _dot(hb, w_ref[:, :w])
    qs_ref[...] = q * _sigmoid(q)
    yield
    f = _dot(hb, w_ref[:, w:2 * w])
    lf_ref[...] = jnp.log(lb + (1.0 - lb) * _sigmoid(f))
    kk_ref[...] = (1.0 - lb) * _sigmoid(-f)
    yield
    v_ref[...] = _dot(hb, w_ref[:, 2 * w:3 * w]).astype(BF16)
    yield
    g = _dot(hb, w_ref[:, 3 * w:4 * w])
    gs_ref[...] = g * _sigmoid(g)
    yield
    xq_ref[...] = (_dot(hb, w_ref[:, 4 * w:]) * (X_DIM ** -0.5 * LOG2E)).astype(BF16)


def _pre1_kernel(*refs, layer):
    for _ in _gate_steps(*refs, layer):
        pass


def _pre1(x, g, w, lb, layer, tm):
    n, d = x.shape
    hw = HG_HEADS * HG_DIM
    xw = X_HEADS * X_DIM
    row = lambda wd: pl.BlockSpec((tm, wd), lambda i: (i, 0))
    f32o = jax.ShapeDtypeStruct((n, hw), F32)
    return pl.pallas_call(
        functools.partial(_pre1_kernel, layer=layer),
        grid=(n // tm,),
        in_specs=[row(d), _const_spec(g.shape), _const_spec(w.shape), _const_spec(lb.shape)],
        out_specs=[row(hw), row(hw), row(hw), row(hw), row(hw), row(xw)],
        out_shape=[f32o, f32o, f32o, jax.ShapeDtypeStruct((n, hw), BF16), f32o,
                   jax.ShapeDtypeStruct((n, xw), BF16)],
        compiler_params=_params("arbitrary"),
        name="pre1",
    )(x, g, w, lb)


def _split3(x):
    hi = x.astype(BF16)
    r1 = x - hi.astype(F32)
    mid = r1.astype(BF16)
    lo = (r1 - mid.astype(F32)).astype(BF16)
    return hi, mid, lo


def _gla_steps(qs_ref, lf_ref, kk_ref, v_ref, gs_ref, gn, mix_ref, row0, cb, st=None,
               s0_ref=None, sfin_ref=None):
    t = cb * CHUNK
    d = HG_DIM
    independent = st is None
    row = lax.broadcasted_iota(jnp.int32, (t, t), 0)
    col = lax.broadcasted_iota(jnp.int32, (t, t), 1)
    causal = jnp.logical_and(row // CHUNK == col // CHUNK, col <= row)
    tri = jnp.where(causal, 1.0, 0.0).astype(BF16)
    hw = HG_HEADS * d
    heads = [slice(h * d, (h + 1) * d) for h in range(HG_HEADS)]

    bb = _dot(tri, jnp.concatenate(_split3(lf_ref[...]), axis=1))
    b = (bb[:, :hw] + bb[:, hw:2 * hw]) + bb[:, 2 * hw:]
    yield
    bcast = lambda r: jnp.concatenate(
        [jnp.broadcast_to(b[c * CHUNK + r:c * CHUNK + r + 1, :], (CHUNK, hw)) for c in range(cb)],
        axis=0)
    ref = bcast(CHUNK // 2)
    last = bcast(CHUNK - 1)
    qs = qs_ref[...]
    kk = kk_ref[...]
    v = v_ref[...]
    qe = (qs * jnp.exp(b - ref)).astype(BF16)
    ke = (kk * jnp.exp(ref - b)).astype(BF16)
    kh = (kk * jnp.exp(last - b)).astype(BF16)
    qb = (qs * jnp.exp(b)).astype(BF16)
    yield
    a = [jnp.where(causal, _dot_nt(qe[:, sl], ke[:, sl]), 0.0).astype(BF16) for sl in heads]
    yield
    o = [_dot(a[h], v[:, sl]) for h, sl in enumerate(heads)]
    yield
    inter = [[] for _ in heads]
    cur = None if independent else list(st)
    for c in range(cb):
        rows = slice(c * CHUNK, (c + 1) * CHUNK)
        decay = jnp.exp(b[c * CHUNK + CHUNK - 1:c * CHUNK + CHUNK, :])
        if independent:
            cur = [s0_ref[c, h].T for h in range(HG_HEADS)]
        for h, sl in enumerate(heads):
            inter[h].append(_dot_nt(qb[rows, sl], cur[h].astype(BF16)))
        cur = [cur[h] * decay[:, sl] + _dot_tn(v[rows, sl], kh[rows, sl])
               for h, sl in enumerate(heads)]
        if independent:
            for h in range(HG_HEADS):
                sfin_ref[c, h] = cur[h].T
    if not independent:
        st[:] = cur
    yield
    for h, sl in enumerate(heads):
        o_h = o[h] + (inter[h][0] if cb == 1 else jnp.concatenate(inter[h], axis=0))
        mix_ref[row0:row0 + t, sl] = (_rms(o_h, gn) * gs_ref[:, sl]).astype(BF16)


def _gla_kernel(qs_ref, lf_ref, kk_ref, v_ref, gs_ref, s0_ref, gn_ref, mix_ref, sfin_ref, *, cb):
    for _ in _gla_steps(qs_ref, lf_ref, kk_ref, v_ref, gs_ref, gn_ref[...], mix_ref, 0, cb,
                        s0_ref=s0_ref, sfin_ref=sfin_ref):
        pass


def _gla(qs, lf, kk, v, gs, s0, gn, cb):
    n, hw = qs.shape
    t = cb * CHUNK
    row = pl.BlockSpec((t, hw), lambda b: (b, 0))
    st = pl.BlockSpec((cb, HG_HEADS, HG_DIM, HG_DIM), lambda b: (b, 0, 0, 0))
    return pl.pallas_call(
        functools.partial(_gla_kernel, cb=cb),
        grid=(n // t,),
        in_specs=[row, row, row, row, row, st, _const_spec(gn.shape)],
        out_specs=[row, st],
        out_shape=[jax.ShapeDtypeStruct((n, hw), BF16),
                   jax.ShapeDtypeStruct(s0.shape, F32)],
        compiler_params=_params("arbitrary"),
        name="gla",
    )(qs, lf, kk, v, gs, s0, gn)


def _hgrn_seq_kernel(*refs, layer, cb, bps):
    x0_ref, x_refs = refs[0], refs[1:1 + bps]
    (g_ref, w_ref, lb_ref, gn_ref, s0_ref, mix_ref, xq_ref, sfin_ref) = refs[1 + bps:9 + bps]
    scratch = refs[9 + bps:]
    slots, st_ref = (scratch[:6], scratch[6:12]), scratch[12]
    i = pl.program_id(0)
    t = cb * CHUNK
    @pl.when(i == 0)
    def _():
        _pre1_kernel(x0_ref, g_ref, w_ref, lb_ref, *slots[0], layer=layer)
        for h in range(HG_HEADS):
            st_ref[h] = s0_ref[0, h].T

    gn = gn_ref[...]
    st = [st_ref[h] for h in range(HG_HEADS)]
    for k in range(bps):
        cur, nxt = slots[k % 2], slots[(k + 1) % 2]
        gates = _gate_steps(x_refs[k], g_ref, w_ref, lb_ref, *nxt, layer)
        stages = _gla_steps(*cur[:5], gn, mix_ref, k * t, cb, st=st)
        for _ in itertools.zip_longest(gates, stages):
            pass
        xq_ref[k * t:(k + 1) * t, :] = cur[5][...]
    for h in range(HG_HEADS):
        st_ref[h] = st[h]

    @pl.when(i == pl.num_programs(0) - 1)
    def _():
        for h in range(HG_HEADS):
            sfin_ref[0, h] = st_ref[h].T


def _hgrn_seq(x, g, w, lb, layer, s0, gn, cb, bps):
    n, d = x.shape
    hw = HG_HEADS * HG_DIM
    xw = X_HEADS * X_DIM
    t = cb * CHUNK
    nblk = n // t
    assert bps % 2 == 0 and nblk % bps == 0 and s0.shape[0] == 1
    xspec = lambda f: pl.BlockSpec((t, d), lambda i: (f(i), 0))
    nxt = lambda k: xspec(lambda i: jnp.minimum(bps * i + k + 1, nblk - 1))
    out = lambda wd: pl.BlockSpec((bps * t, wd), lambda i: (i, 0))
    st = pl.BlockSpec((1, HG_HEADS, HG_DIM, HG_DIM), lambda i: (0, 0, 0, 0))
    slot = [pltpu.VMEM((t, hw), F32), pltpu.VMEM((t, hw), F32), pltpu.VMEM((t, hw), F32),
            pltpu.VMEM((t, hw), BF16), pltpu.VMEM((t, hw), F32), pltpu.VMEM((t, xw), BF16)]
    return pl.pallas_call(
        functools.partial(_hgrn_seq_kernel, layer=layer, cb=cb, bps=bps),
        grid=(nblk // bps,),
        in_specs=[xspec(lambda i: 0)] + [nxt(k) for k in range(bps)]
                 + [_const_spec(g.shape), _const_spec(w.shape), _const_spec(lb.shape),
                    _const_spec(gn.shape), st],
        out_specs=[out(hw), out(xw), st],
        out_shape=[jax.ShapeDtypeStruct((n, hw), BF16), jax.ShapeDtypeStruct((n, xw), BF16),
                   jax.ShapeDtypeStruct(s0.shape, F32)],
        scratch_shapes=slot + slot + [pltpu.VMEM((HG_HEADS, HG_DIM, HG_DIM), F32)],
        compiler_params=_params("arbitrary"),
        name="hgrn_seq",
    )(*([x] * (bps + 1)), g, w, lb, gn, s0)


def _rope_tables(tile_pos, tile_offsets):
    half = MLA_ROPE // 2
    inv = jnp.power(ROPE_THETA, -jnp.arange(half, dtype=F32) / half)

    def cos_sin(pos):
        ang = pos.astype(F32)[:, None] * inv[None, :]
        return jnp.cos(ang), jnp.sin(ang)

    pat = lambda a: jnp.concatenate([a, a, jnp.zeros_like(a), jnp.zeros_like(a)], axis=1)
    cos, sin = cos_sin(tile_pos)
    cos0, sin0 = cos_sin(tile_offsets)
    wide = lambda a: jnp.broadcast_to(a[:, :, None], a.shape + (LANES,))
    return (pat(cos), pat(sin), pat(cos0)[:, None, :], pat(sin0)[:, None, :],
            cos.T, sin.T, wide(cos0), wide(sin0))


def _prep_weights(mla_w_in, mla_w_uq, mla_w_uk, mla_w_uv):
    o1 = MLA_Q_RANK + MLA_KV_RANK
    o2 = o1 + MLA_ROPE
    d = mla_w_in.shape[0]
    win = jnp.concatenate([mla_w_in[:, :o1], mla_w_in[:, o2:], mla_w_in[:, o1:o2],
                           jnp.zeros((d, LANES - MLA_ROPE), mla_w_in.dtype)], axis=1)
    wq = mla_w_uq.reshape(MLA_Q_RANK, MLA_HEADS, MLA_NOPE + MLA_ROPE)
    nope = wq[:, :, :MLA_NOPE].reshape(MLA_Q_RANK, MLA_HEADS * MLA_NOPE)
    rope = jnp.pad(wq[:, :, MLA_NOPE:], ((0, 0), (0, 0), (0, LANES - MLA_ROPE)))
    wuq = jnp.concatenate([nope, rope.reshape(MLA_Q_RANK, MLA_HEADS * LANES)], axis=1)
    wuk = mla_w_uk.reshape(MLA_KV_RANK, MLA_HEADS * MLA_NOPE)
    wuv = mla_w_uv.reshape(MLA_KV_RANK, MLA_HEADS * MLA_V)
    wq_t = wq.reshape(MLA_Q_RANK, -1).T
    bf = lambda a: a.astype(BF16)
    return bf(win), bf(wuq), bf(wuk), bf(wuv), bf(wq_t), bf(wuv.T)


def kernel(x_prompt, x_sample, cache_mla_latent, cache_mla_krope, cache_hgrn_state, cache_mem_k,
           cache_mem_v, mem_prompt, ln_mix_pre, ln_mix_post, ln_ffn_pre, ln_ffn_post, mem_norm,
           w_mem_kv, mla_w_in, mla_q_norm, mla_kv_norm, mla_w_uq, mla_w_uk, mla_w_uv, mla_w_out,
           hgrn_w_in, hgrn_lb, hgrn_o_norm, hgrn_w_out, w_ffn_up, w_ffn_down):
    bp, tp, d = x_prompt.shape
    bs, ts, _ = x_sample.shape
    depth = ln_mix_pre.shape[0]
    past = cache_mla_latent.shape[2]
    n_mem = mem_prompt.shape[1]
    assert depth == 2 and bp == 1 and ts == CHUNK and past % CHUNK == 0
    assert MLA_NOPE == LANES and MLA_V == LANES and HG_DIM == LANES and X_DIM == LANES

    row2 = lambda a: a.reshape(1, -1)
    win0, wuq, wuk, wuv, wuq_t, wuv_t = _prep_weights(mla_w_in[0], mla_w_uq[0], mla_w_uk[0],
                                                      mla_w_uv[0])
    wout = (mla_w_out[0].astype(BF16), hgrn_w_out[0].astype(BF16))
    wup = w_ffn_up.astype(BF16)
    wdn = w_ffn_down.astype(BF16)
    win1 = hgrn_w_in[0].astype(BF16)

    mem_k_p, mem_v_p = _memkv(mem_prompt, mem_norm, w_mem_kv.astype(BF16))

    def trunk(x, pos, mem_k, mem_v, s0, nbatch, tm, tb, history):
        n = x.shape[0]
        tables = _rope_tables(*pos)
        pre0_args = (x, row2(ln_mix_pre[0]), win0, row2(mla_q_norm[0]), row2(mla_kv_norm[0]))
        if history is None:
            qt, kn, kr, vt3, lat, krope, xq = _pre0(
                *pre0_args, wuq_t, tables, ATTN_TILE, (wuk, wuv_t))
            mix = _attn_prompt(qt, kn, kr, vt3, ATTN_TILE, ATTN_HEADS_PER_STEP)
        else:
            qn, qr, lat, krope, xq = _pre0(*pre0_args, wuq, tables, tm)
            mix = _attn_sample(qn, qr, history[0], history[1], lat, krope, wuk, wuv, ts,
                               SAMPLE_ATTN_BATCH)
        x = _post(x, mix, xq, mem_k, mem_v, 0, wout[0],
                  row2(ln_mix_post[0]), row2(ln_ffn_pre[0]), wup, wdn,
                  row2(ln_ffn_post[0]), tm, tb)
        if n // nbatch == CHUNK:
            qs, lf, kk, vv, gs, xq = _pre1(x, row2(ln_mix_pre[1]), win1, hgrn_lb, 1, PRE1_TILE)
            mix, s_fin = _gla(qs, lf, kk, vv, gs, s0, row2(hgrn_o_norm[0]), GLA_CHUNKS)
        else:
            mix, xq, s_fin = _hgrn_seq(x, row2(ln_mix_pre[1]), win1, hgrn_lb, 1, s0,
                                       row2(hgrn_o_norm[0]), GLA_CHUNKS, HGRN_BLOCKS_PER_STEP)
        x = _post(x, mix, xq, mem_k, mem_v, 1, wout[1],
                  row2(ln_mix_post[1]), row2(ln_ffn_pre[1]), wup, wdn,
                  row2(ln_ffn_post[1]), tm, tb)
        return x, lat, krope, s_fin

    s0_p = jnp.zeros((bp, HG_HEADS, HG_DIM, HG_DIM), F32)
    pos_p = (jnp.arange(ATTN_TILE), ATTN_TILE * jnp.arange(bp * tp // ATTN_TILE))
    y_p, lat_p, kr_p, st_p = trunk(x_prompt.reshape(bp * tp, d), pos_p, mem_k_p, mem_v_p,
                                   s0_p, bp, TOKEN_TILE, TOKEN_TILE, None)
    pos_s = (jnp.tile(past + jnp.arange(ts), TOKEN_TILE // ts),
             jnp.zeros((bs * ts // TOKEN_TILE,), jnp.int32))
    mk_s = cache_mem_k.reshape(depth, bs, n_mem * X_HEADS, X_DIM)
    mv_s = cache_mem_v.reshape(depth, bs, n_mem * X_HEADS, X_DIM)
    y_s, lat_s, kr_s, st_s = trunk(x_sample.reshape(bs * ts, d), pos_s, mk_s, mv_s,
                                   cache_hgrn_state[0], bs, TOKEN_TILE, ts,
                                   (cache_mla_latent[0], jnp.swapaxes(cache_mla_krope[0], 1, 2)))

    return (y_p.reshape(bp, tp, d), y_s.reshape(bs, ts, d),
            lat_p.reshape(1, bp, tp, MLA_KV_RANK), kr_p.T.reshape(1, bp, tp, MLA_ROPE),
            st_p.reshape(1, bp, HG_HEADS, HG_DIM, HG_DIM),
            mem_k_p.reshape(depth, bp, n_mem, X_HEADS, X_DIM),
            mem_v_p.reshape(depth, bp, n_mem, X_HEADS, X_DIM),
            lat_s.reshape(1, bs, ts, MLA_KV_RANK), kr_s.T.reshape(1, bs, ts, MLA_ROPE),
            st_s.reshape(1, bs, HG_HEADS, HG_DIM, HG_DIM))
```
